```python
import math
import jax, jax.numpy as jnp
from jax import lax
import numpy as np

D_MODEL = 1024
BATCH = 8
SEQ = 8192
DEPTH = 4
DEC_BATCH = 8
DEC_SEQ = 16
PAST_LEN = 4096

CHUNK = 64
D_FF = 2816
EPS = 1e-6
POOL_W = D_MODEL // 2
POOL_GROUPS = 4
POOL_GC = POOL_W // POOL_GROUPS
POOL_WINDOWS = (2, 4, 8, 16)
POOL_HIST = 15
SSM_INNER = D_MODEL
SSM_HEADDIM = 64
SSM_HEADS = SSM_INNER // SSM_HEADDIM
SSM_GROUPS = 4
SSM_STATE = 128
SSM_CONV = 4
XBC_W = SSM_INNER + 2 * SSM_GROUPS * SSM_STATE
GMLP_W = D_MODEL // 2
GMLP_HEADS = 4
GMLP_HC = GMLP_W // GMLP_HEADS
GMLP_CHUNK = 128
N_BRANCH = 3
IN_COLS = POOL_W + SSM_INNER + XBC_W + SSM_HEADS + 2 * GMLP_W + N_BRANCH * D_MODEL

kernel_name = 'hybrid_streaming_encoder_step'


def rms_norm(x, g):
    xf = x.astype(jnp.float32)
    y = xf * lax.rsqrt(jnp.mean(xf * xf, axis=-1, keepdims=True) + EPS)
    return (y * g.astype(jnp.float32)).astype(x.dtype)


def layer_norm(x, g, b):
    xf = x.astype(jnp.float32)
    mu = jnp.mean(xf, axis=-1, keepdims=True)
    xc = xf - mu
    y = xc * lax.rsqrt(jnp.mean(xc * xc, axis=-1, keepdims=True) + EPS)
    return (y * g.astype(jnp.float32) + b.astype(jnp.float32)).astype(x.dtype)


def swiglu(x, w_gu, w_down):
    gate, up = jnp.split(x @ w_gu, 2, axis=-1)
    return (jax.nn.silu(gate) * up) @ w_down


def causal_dwconv(x, hist, w, b):
    L = x.shape[1]
    xp = jnp.concatenate([hist, x], axis=1)
    out = b + w[0] * xp[:, 0:L]
    for k in range(1, SSM_CONV):
        out = out + w[k] * xp[:, k:k + L]
    return out, xp[:, L:]


def pool_mixer(xa, hist, pos0, pool_w, pool_scale):
    b, L, C = xa.shape
    ext = jnp.concatenate([hist, xa], axis=1)
    cs = jnp.cumsum(ext.astype(jnp.float32), axis=1)
    cs = jnp.concatenate([jnp.zeros_like(cs[:, :1]), cs], axis=1)
    off = POOL_HIST + 1
    pos = pos0 + jnp.arange(L)
    means = []
    for gi, w in enumerate(POOL_WINDOWS):
        sl = slice(gi * POOL_GC, (gi + 1) * POOL_GC)
        s = cs[:, off:, sl] - cs[:, off - w:off - w + L, sl]
        cnt = jnp.minimum(pos + 1, w).astype(jnp.float32)[None, :, None]
        means.append(s / cnt)
    mean = jnp.concatenate(means, axis=-1).astype(xa.dtype)
    zz = (mean - xa).reshape(b, L, POOL_GROUPS, POOL_GC)
    y = jnp.einsum('blgc,gcd->blgd', zz, pool_w).reshape(b, L, C) * pool_scale
    return y, ext[:, L:]


def ssd_scan(x, dt, a, bm, cm, s0, blk):
    b, L, H, P = x.shape
    G, N = bm.shape[2], bm.shape[3]
    R = H // G
    nc = L // blk
    f32 = jnp.float32
    xf = x.astype(f32).reshape(b, nc, blk, G, R, P)
    dtf = dt.astype(f32).reshape(b, nc, blk, G, R)
    bf = bm.astype(f32).reshape(b, nc, blk, G, N)
    cf = cm.astype(f32).reshape(b, nc, blk, G, N)
    cum = jnp.cumsum(dtf * a.reshape(G, R), axis=2)
    cum_t = jnp.moveaxis(cum, 2, -1)
    mask = jnp.tril(jnp.ones((blk, blk), dtype=bool))
    decay = jnp.exp(jnp.where(mask, cum_t[..., :, None] - cum_t[..., None, :], -jnp.inf))
    cb = jnp.einsum('bclgn,bcsgn->bcgls', cf, bf)
    mmat = cb[:, :, :, None] * decay * jnp.moveaxis(dtf, 2, -1)[..., None, :]
    y_diag = jnp.einsum('bcgrls,bcsgrp->bclgrp', mmat, xf)
    last = cum[:, :, -1]
    xw = xf * (jnp.exp(last[:, :, None] - cum) * dtf)[..., None]
    chunk_states = jnp.einsum('bclgn,bclgrp->bcgrpn', bf, xw)

    def step(s, inp):
        dec, st = inp
        return s * dec[..., None, None] + st, s

    s_fin, s_in = lax.scan(step, s0.astype(f32).reshape(b, G, R, P, N),
                           (jnp.moveaxis(jnp.exp(last), 1, 0), jnp.moveaxis(chunk_states, 1, 0)))
    s_in = jnp.moveaxis(s_in, 0, 1)
    y_off = jnp.einsum('bclgn,bcgrpn->bclgrp', cf, s_in) * jnp.exp(cum)[..., None]
    y = (y_diag + y_off).reshape(b, L, H, P).astype(x.dtype)
    return y, s_fin.reshape(b, H, P, N).astype(s0.dtype)


def spatial_gate(u, vn, ws, bs):
    b, L, C = vn.shape
    blk = min(L, GMLP_CHUNK)
    n = L // blk
    mask = jnp.tril(jnp.ones((blk, blk), dtype=bool))
    wsm = jnp.where(mask, ws[:, :blk, :blk], 0.0).astype(vn.dtype)
    vh = vn.reshape(b, n, blk, GMLP_HEADS, GMLP_HC)
    sv = jnp.einsum('hts,bnshc->bnthc', wsm, vh) + bs[:, :blk].T[:, :, None].astype(vn.dtype)
    return u * sv.reshape(b, L, C)


def mixer(xn, hist_pool, hist_conv, s0, pos0, w_in, pool_w, pool_scale, conv_w, conv_b, dt_bias,
          a_log, d_skip, norm_g, gn_g, gn_b, ws, bs, wa, wb, wc, wo):
    b, L, _ = xn.shape
    proj = xn @ w_in
    cuts = np.cumsum([POOL_W, SSM_INNER, XBC_W, SSM_HEADS, GMLP_W, GMLP_W]).tolist()
    xa, z, xbc, dt_raw, u, v, gate_cols = jnp.split(proj, cuts, axis=-1)
    ya, new_pool = pool_mixer(xa, hist_pool, pos0, pool_w, pool_scale)
    xbc, new_conv = causal_dwconv(xbc, hist_conv, conv_w, conv_b)
    xbc = jax.nn.silu(xbc)
    xs, bm, cm = jnp.split(xbc, [SSM_INNER, SSM_INNER + SSM_GROUPS * SSM_STATE], axis=-1)
    xs = xs.reshape(b, L, SSM_HEADS, SSM_HEADDIM)
    bm = bm.reshape(b, L, SSM_GROUPS, SSM_STATE)
    cm = cm.reshape(b, L, SSM_GROUPS, SSM_STATE)
    dt = jax.nn.softplus(dt_raw.astype(jnp.float32) + dt_bias.astype(jnp.float32))
    a = -jnp.exp(a_log.astype(jnp.float32))
    y, new_ssm = ssd_scan(xs, dt, a, bm, cm, s0, min(L, CHUNK))
    y = y + d_skip[:, None].astype(y.dtype) * xs
    yb = rms_norm(y.reshape(b, L, SSM_INNER) * jax.nn.silu(z), norm_g)
    vn = layer_norm(v, gn_g, gn_b)
    yc = spatial_gate(u, vn, ws, bs)
    gates = jax.nn.sigmoid(gate_cols.astype(jnp.float32)).astype(xn.dtype).reshape(b, L, N_BRANCH, D_MODEL)
    merged = gates[:, :, 0] * (ya @ wa) + gates[:, :, 1] * (yb @ wb) + gates[:, :, 2] * (yc @ wc)
    return merged @ wo, new_pool, new_conv, new_ssm, vn


def run_trunk(x, pool_st, conv_st, ssm_st, pos0,
              ffn1_pre_g, ffn1_post_g, ffn1_w_gu, ffn1_w_down,
              mix_pre_g, mix_post_g, w_in, pool_w, pool_scale,
              ssm_conv_w, ssm_conv_b, ssm_dt_bias, ssm_a_log, ssm_d, ssm_norm_g,
              gmlp_norm_g, gmlp_norm_b, gmlp_ws, gmlp_bs,
              w_branch_a, w_branch_b, w_branch_c, w_out,
              ffn2_pre_g, ffn2_post_g, ffn2_w_gu, ffn2_w_down):
    new_pool, new_conv, new_ssm, new_v = [], [], [], []
    h = x
    for l in range(DEPTH):
        f = swiglu(rms_norm(h, ffn1_pre_g[l]), ffn1_w_gu[l], ffn1_w_down[l])
        h = h + 0.5 * rms_norm(f, ffn1_post_g[l])
        m, ps, cs, ss, vn = mixer(rms_norm(h, mix_pre_g[l]), pool_st[l], conv_st[l], ssm_st[l], pos0,
                                  w_in[l], pool_w[l], pool_scale[l], ssm_conv_w[l], ssm_conv_b[l],
                                  ssm_dt_bias[l], ssm_a_log[l], ssm_d[l], ssm_norm_g[l],
                                  gmlp_norm_g[l], gmlp_norm_b[l], gmlp_ws[l], gmlp_bs[l],
                                  w_branch_a[l], w_branch_b[l], w_branch_c[l], w_out[l])
        h = h + rms_norm(m, mix_post_g[l])
        f = swiglu(rms_norm(h, ffn2_pre_g[l]), ffn2_w_gu[l], ffn2_w_down[l])
        h = h + 0.5 * rms_norm(f, ffn2_post_g[l])
        new_pool.append(ps)
        new_conv.append(cs)
        new_ssm.append(ss)
        new_v.append(vn)
    return h, jnp.stack(new_pool), jnp.stack(new_conv), jnp.stack(new_ssm), new_v


def setup_inputs(seed: int = 0) -> dict:
    key = jax.random.key(seed)
    ks = jax.random.split(key, 32)
    f32 = jnp.float32

    def nrm(k, shape, scale):
        return scale * jax.random.normal(k, shape, f32)

    def gain(k, shape):
        return 1.0 + 0.05 * jax.random.normal(k, shape, f32)

    dt0 = jnp.exp(jax.random.uniform(ks[16], (DEPTH, SSM_HEADS), f32, math.log(1e-3), math.log(1e-1)))
    return {
        'x_prompt': nrm(ks[0], (BATCH, SEQ, D_MODEL), 1.0),
        'x_sample': nrm(ks[1], (DEC_BATCH, DEC_SEQ, D_MODEL), 1.0),
        'state_pool': nrm(ks[2], (DEPTH, DEC_BATCH, POOL_HIST, POOL_W), 1.0),
        'state_conv': nrm(ks[3], (DEPTH, DEC_BATCH, SSM_CONV - 1, XBC_W), 1.0),
        'state_ssm': nrm(ks[4], (DEPTH, DEC_BATCH, SSM_HEADS, SSM_HEADDIM, SSM_STATE), 0.5),
        'ffn1_pre_g': gain(ks[5], (DEPTH, D_MODEL)),
        'ffn1_post_g': gain(ks[6], (DEPTH, D_MODEL)),
        'ffn1_w_gu': nrm(ks[7], (DEPTH, D_MODEL, 2 * D_FF), D_MODEL ** -0.5),
        'ffn1_w_down': nrm(ks[8], (DEPTH, D_FF, D_MODEL), D_FF ** -0.5),
        'mix_pre_g': gain(ks[9], (DEPTH, D_MODEL)),
        'mix_post_g': gain(ks[10], (DEPTH, D_MODEL)),
        'w_in': nrm(ks[11], (DEPTH, D_MODEL, IN_COLS), D_MODEL ** -0.5),
        'pool_w': nrm(ks[12], (DEPTH, POOL_GROUPS, POOL_GC, POOL_GC), POOL_GC ** -0.5),
        'pool_scale': gain(ks[13], (DEPTH, POOL_W)),
        'ssm_conv_w': nrm(ks[14], (DEPTH, SSM_CONV, XBC_W), SSM_CONV ** -0.5),
        'ssm_conv_b': nrm(ks[15], (DEPTH, XBC_W), 0.01),
        'ssm_dt_bias': dt0 + jnp.log(-jnp.expm1(-dt0)),
        'ssm_a_log': jnp.log(jax.random.uniform(ks[17], (DEPTH, SSM_HEADS), f32, 1.0, 16.0)),
        'ssm_d': gain(ks[18], (DEPTH, SSM_HEADS)),
        'ssm_norm_g': gain(ks[19], (DEPTH, SSM_INNER)),
        'gmlp_norm_g': gain(ks[20], (DEPTH, GMLP_W)),
        'gmlp_norm_b': nrm(ks[21], (DEPTH, GMLP_W), 0.01),
        'gmlp_ws': nrm(ks[22], (DEPTH, GMLP_HEADS, GMLP_CHUNK, GMLP_CHUNK), GMLP_CHUNK ** -0.5),
        'gmlp_bs': gain(ks[23], (DEPTH, GMLP_HEADS, GMLP_CHUNK)),
        'w_branch_a': nrm(ks[24], (DEPTH, POOL_W, D_MODEL), POOL_W ** -0.5),
        'w_branch_b': nrm(ks[25], (DEPTH, SSM_INNER, D_MODEL), SSM_INNER ** -0.5),
        'w_branch_c': nrm(ks[26], (DEPTH, GMLP_W, D_MODEL), GMLP_W ** -0.5),
        'w_out': nrm(ks[27], (DEPTH, D_MODEL, D_MODEL), D_MODEL ** -0.5),
        'ffn2_pre_g': gain(ks[28], (DEPTH, D_MODEL)),
        'ffn2_post_g': gain(ks[29], (DEPTH, D_MODEL)),
        'ffn2_w_gu': nrm(ks[30], (DEPTH, D_MODEL, 2 * D_FF), D_MODEL ** -0.5),
        'ffn2_w_down': nrm(ks[31], (DEPTH, D_FF, D_MODEL), D_FF ** -0.5),
    }


def reference(x_prompt, x_sample, state_pool, state_conv, state_ssm,
              ffn1_pre_g, ffn1_post_g, ffn1_w_gu, ffn1_w_down,
              mix_pre_g, mix_post_g, w_in, pool_w, pool_scale,
              ssm_conv_w, ssm_conv_b, ssm_dt_bias, ssm_a_log, ssm_d, ssm_norm_g,
              gmlp_norm_g, gmlp_norm_b, gmlp_ws, gmlp_bs,
              w_branch_a, w_branch_b, w_branch_c, w_out,
              ffn2_pre_g, ffn2_post_g, ffn2_w_gu, ffn2_w_down):
    weights = (ffn1_pre_g, ffn1_post_g, ffn1_w_gu, ffn1_w_down,
               mix_pre_g, mix_post_g, w_in, pool_w, pool_scale,
               ssm_conv_w, ssm_conv_b, ssm_dt_bias, ssm_a_log, ssm_d, ssm_norm_g,
               gmlp_norm_g, gmlp_norm_b, gmlp_ws, gmlp_bs,
               w_branch_a, w_branch_b, w_branch_c, w_out,
               ffn2_pre_g, ffn2_post_g, ffn2_w_gu, ffn2_w_down)
    bp = x_prompt.shape[0]
    zero_pool = jnp.zeros((DEPTH, bp, POOL_HIST, POOL_W), x_prompt.dtype)
    zero_conv = jnp.zeros((DEPTH, bp, SSM_CONV - 1, XBC_W), x_prompt.dtype)
    zero_ssm = jnp.zeros((DEPTH, bp, SSM_HEADS, SSM_HEADDIM, SSM_STATE), x_prompt.dtype)
    y_prompt, pool_p, conv_p, ssm_p, _ = run_trunk(x_prompt, zero_pool, zero_conv, zero_ssm, 0, *weights)
    y_sample, pool_s, conv_s, ssm_s, v_rows = run_trunk(x_sample, state_pool, state_conv, state_ssm, PAST_LEN, *weights)
    gmlp_v_s = jnp.stack(v_rows)
    return (y_prompt, y_sample, pool_p, conv_p, ssm_p, pool_s, conv_s, ssm_s, gmlp_v_s)
```

```python
import functools

import numpy as np
import jax
import jax.numpy as jnp
from jax import lax
from jax.experimental import pallas as pl
from jax.experimental.pallas import tpu as pltpu

D_MODEL = 1024
DEPTH = 4
D_FF = 2816
EPS = 1e-6
POOL_W = 512
POOL_GROUPS = 4
POOL_GC = 128
POOL_WINDOWS = (2, 4, 8, 16)
POOL_HIST = 15
SSM_INNER = 1024
SSM_HEADDIM = 64
SSM_HEADS = 16
SSM_GROUPS = 4
SSM_STATE = 128
SSM_CONV = 4
XBC_W = 2048
GMLP_W = 512
GMLP_HEADS = 4
GMLP_HC = 128
GMLP_CHUNK = 128
PAST_LEN = 4096

LANES = 128
POOL_PAD = 16
CONV_PAD = 8
SSD_BLOCK = 128
FF_COLS = 256
VMEM_LIMIT = 56 * 1024 * 1024

C_XA = 0
C_Z = C_XA + POOL_W
C_XBC = C_Z + SSM_INNER
C_U = C_XBC + XBC_W
C_V = C_U + GMLP_W
C_GA = C_V + GMLP_W
C_GB = C_GA + D_MODEL
C_GC = C_GB + D_MODEL
C_DT = C_GC + D_MODEL
IN_COLS_R = C_DT + LANES

BF = jnp.bfloat16
F32 = jnp.float32


def _dot(a, b):
    return jnp.dot(a, b, preferred_element_type=F32)


def _dot_nt(a, b):
    return lax.dot_general(a, b, (((1,), (1,)), ((), ())), preferred_element_type=F32)


def _dot_tn(a, b):
    return lax.dot_general(a, b, (((0,), (0,)), ((), ())), preferred_element_type=F32)


def _rms(x, g):
    return x * lax.rsqrt(jnp.mean(x * x, axis=-1, keepdims=True) + EPS) * g


def _sigmoid(x):
    return 1.0 / (1.0 + jnp.exp(-x))


def _silu(x):
    return x * _sigmoid(x)


def _softplus(x):
    return jnp.maximum(x, 0.0) + jnp.log1p(jnp.exp(-jnp.abs(x)))


def _div_pow2(i, d):
    assert d & (d - 1) == 0
    return jnp.right_shift(i, d.bit_length() - 1)


def _split_bf16(x, parts):
    out = []
    for _ in range(parts - 1):
        p = x.astype(BF)
        out.append(p)
        x = x - p.astype(F32)
    out.append(x.astype(BF))
    return out


def _ffn_body(x_ref, pre_ref, post_ref, wgu_ref, wd_ref, o_ref, h_ref):
    x = x_ref[...]
    xn = _rms(x, pre_ref[...]).astype(BF)
    for j in range(D_FF // FF_COLS):
        g = _dot(xn, wgu_ref[:, j * FF_COLS:(j + 1) * FF_COLS])
        u = _dot(xn, wgu_ref[:, D_FF + j * FF_COLS:D_FF + (j + 1) * FF_COLS])
        h_ref[:, j * FF_COLS:(j + 1) * FF_COLS] = (_silu(g) * u).astype(BF)
    f = _dot(h_ref[...], wd_ref[...])
    o_ref[...] = x + 0.5 * _rms(f, post_ref[...])


def _resident(shape, layer):
    nd = len(shape)
    return pl.BlockSpec((None,) + tuple(shape), lambda *_: (layer,) + (0,) * nd,
                        pipeline_mode=pl.Buffered(1))


def _ffn_call(x2d, pre_g, post_g, w_gu, w_down, layer, tm):
    n = x2d.shape[0]
    return pl.pallas_call(
        _ffn_body,
        grid=(n // tm,),
        in_specs=[
            pl.BlockSpec((tm, D_MODEL), lambda i: (i, 0)),
            _resident((1, D_MODEL), layer),
            _resident((1, D_MODEL), layer),
            _resident((D_MODEL, 2 * D_FF), layer),
            _resident((D_FF, D_MODEL), layer),
        ],
        out_specs=pl.BlockSpec((tm, D_MODEL), lambda i: (i, 0)),
        out_shape=jax.ShapeDtypeStruct((n, D_MODEL), F32),
        scratch_shapes=[pltpu.VMEM((tm, D_FF), BF)],
        compiler_params=pltpu.CompilerParams(
            dimension_semantics=("parallel",), vmem_limit_bytes=VMEM_LIMIT),
        name="swiglu_block",
    )(x2d, pre_g, post_g, w_gu, w_down)


def _mixer_body(nseq, ls, pos0, emit_vn,
                x_ref, pool_in, conv_in, ssm_in, pre_ref, post_ref, win_ref,
                poolw_ref, pscale_ref, convw_ref, convb_ref, dtb_ref, alog_ref, dskip_ref,
                normg_ref, gng_ref, gnb_ref, ws_ref, bs_ref, wa_ref, wb_ref, wc_ref, wo_ref,
                *rest):
    if emit_vn:
        o_ref, pool_out, conv_out, ssm_out, vn_out = rest[:5]
        ext_pool, ext_conv, st_ref, act_ref, y_ref = rest[5:]
    else:
        o_ref, pool_out, conv_out, ssm_out = rest[:4]
        vn_out = None
        ext_pool, ext_conv, st_ref, act_ref, y_ref = rest[4:]

    rows = nseq * ls
    t = pl.program_id(1)
    seg = min(ls, SSD_BLOCK)

    @pl.when(t == 0)
    def _():
        ext_pool[:, 0:POOL_PAD, :] = pool_in[...]
        ext_conv[:, 0:CONV_PAD, :] = conv_in[...]
        for s in range(nseq):
            st_ref[s] = ssm_in[s].T

    x = x_ref[...].reshape(rows, D_MODEL)
    xn = _rms(x, pre_ref[...]).astype(BF)

    xa = _dot(xn, win_ref[:, C_XA:C_XA + POOL_W])
    for s in range(nseq):
        ext_pool[s, POOL_PAD:POOL_PAD + ls, :] = xa[s * ls:(s + 1) * ls]
    pos = pos0 + t * ls + lax.broadcasted_iota(jnp.int32, (ls, 1), 0)
    zz_rows = []
    for s in range(nseq):
        zz_g = []
        for gi, w in enumerate(POOL_WINDOWS):
            c0 = gi * POOL_GC
            acc = ext_pool[s, POOL_PAD:POOL_PAD + ls, c0:c0 + POOL_GC]
            for k in range(1, w):
                acc = acc + ext_pool[s, POOL_PAD - k:POOL_PAD - k + ls, c0:c0 + POOL_GC]
            inv = 1.0 / jnp.minimum(pos + 1, w).astype(F32)
            zz_g.append(acc * inv - ext_pool[s, POOL_PAD:POOL_PAD + ls, c0:c0 + POOL_GC])
        zz_rows.append(jnp.concatenate(zz_g, axis=1))
    zz = zz_rows[0] if nseq == 1 else jnp.concatenate(zz_rows, axis=0)
    ya = _dot(zz.astype(BF), poolw_ref[...]) * pscale_ref[...]
    merged = _sigmoid(_dot(xn, win_ref[:, C_GA:C_GA + D_MODEL])) * _dot(ya.astype(BF), wa_ref[...])

    u = _dot(xn, win_ref[:, C_U:C_U + GMLP_W])
    v = _dot(xn, win_ref[:, C_V:C_V + GMLP_W])
    mu = jnp.mean(v, axis=-1, keepdims=True)
    vc = v - mu
    vn = vc * lax.rsqrt(jnp.mean(vc * vc, axis=-1, keepdims=True) + EPS) * gng_ref[...] + gnb_ref[...]
    if emit_vn:
        vn_out[...] = vn.reshape(nseq, ls, GMLP_W)
    vn_bf = vn.astype(BF)
    gq = min(ls, GMLP_CHUNK)
    tri_g = (lax.broadcasted_iota(jnp.int32, (gq, gq), 0)
             >= lax.broadcasted_iota(jnp.int32, (gq, gq), 1))
    wsm = [jnp.where(tri_g, ws_ref[h, 0:gq, 0:gq], jnp.zeros((), BF)) for h in range(GMLP_HEADS)]
    bias_g = bs_ref[0:gq, :]
    sv_rows = []
    for c in range(rows // gq):
        sv_h = [_dot(wsm[h], vn_bf[c * gq:(c + 1) * gq, h * GMLP_HC:(h + 1) * GMLP_HC])
                for h in range(GMLP_HEADS)]
        sv_rows.append(jnp.concatenate(sv_h, axis=1) + bias_g)
    sv = sv_rows[0] if len(sv_rows) == 1 else jnp.concatenate(sv_rows, axis=0)
    yc = u * sv
    merged = merged + (_sigmoid(_dot(xn, win_ref[:, C_GC:C_GC + D_MODEL]))
                       * _dot(yc.astype(BF), wc_ref[...]))

    xbc_pre = _dot(xn, win_ref[:, C_XBC:C_XBC + XBC_W])
    for s in range(nseq):
        ext_conv[s, CONV_PAD:CONV_PAD + ls, :] = xbc_pre[s * ls:(s + 1) * ls]
    for s in range(nseq):
        base = CONV_PAD - (SSM_CONV - 1)
        cv = convb_ref[...] + convw_ref[0:1, :] * ext_conv[s, base:base + ls, :]
        for k in range(1, SSM_CONV):
            cv = cv + convw_ref[k:k + 1, :] * ext_conv[s, base + k:base + k + ls, :]
        act_ref[s * ls:(s + 1) * ls, :] = _silu(cv)

    dt = _softplus(_dot(xn, win_ref[:, C_DT:C_DT + LANES]) + dtb_ref[...])
    d_a = dt * (-jnp.exp(alog_ref[...]))
    ri = lax.broadcasted_iota(jnp.int32, (rows, rows), 0)
    ci = lax.broadcasted_iota(jnp.int32, (rows, rows), 1)
    seg_tri = jnp.where((_div_pow2(ri, seg) == _div_pow2(ci, seg)) & (ri >= ci), 1.0, 0.0).astype(BF)
    cum = None
    for p in _split_bf16(d_a, 3):
        part = _dot(seg_tri, p)
        cum = part if cum is None else cum + part
    ei = lax.broadcasted_iota(jnp.int32, (LANES, SSM_INNER), 0)
    ej = lax.broadcasted_iota(jnp.int32, (LANES, SSM_INNER), 1)
    expand = jnp.where(_div_pow2(ej, SSM_HEADDIM) == ei, 1.0, 0.0).astype(BF)

    def _expand(a):
        hi, lo = _split_bf16(a, 2)
        return _dot(hi, expand) + _dot(lo, expand)

    cum_e = _expand(cum)
    dt_e = _expand(dt)
    cum_t = cum.T
    dt_t = dt.T

    bi = lax.broadcasted_iota(jnp.int32, (SSD_BLOCK, SSD_BLOCK), 0)
    bj = lax.broadcasted_iota(jnp.int32, (SSD_BLOCK, SSD_BLOCK), 1)
    blk_tri = (_div_pow2(bi, seg) == _div_pow2(bj, seg)) & (bi >= bj)
    lane_lo = lax.broadcasted_iota(jnp.int32, (SSD_BLOCK, LANES), 1) < SSM_HEADDIM
    nblk = rows // SSD_BLOCK
    segs_per_blk = SSD_BLOCK // seg
    for b in range(nblk):
        r0 = b * SSD_BLOCK
        xs_bf = act_ref[r0:r0 + SSD_BLOCK, 0:SSM_INNER].astype(BF)
        bm_bf = act_ref[r0:r0 + SSD_BLOCK, SSM_INNER:SSM_INNER + SSM_GROUPS * SSM_STATE].astype(BF)
        cm_bf = act_ref[r0:r0 + SSD_BLOCK, SSM_INNER + SSM_GROUPS * SSM_STATE:XBC_W].astype(BF)
        for g in range(SSM_GROUPS):
            c_g = cm_bf[:, g * SSM_STATE:(g + 1) * SSM_STATE]
            b_g = bm_bf[:, g * SSM_STATE:(g + 1) * SSM_STATE]
            cb = jnp.where(blk_tri, _dot_nt(c_g, b_g), 0.0)
            for j in range(2):
                col = (2 * g + j) * LANES
                xs_pair = xs_bf[:, col:col + LANES]
                yd = []
                for hh in range(2):
                    h = 4 * g + 2 * j + hh
                    diff = cum[r0:r0 + SSD_BLOCK, h:h + 1] - cum_t[h:h + 1, r0:r0 + SSD_BLOCK]
                    dec = jnp.exp(jnp.where(blk_tri, diff, -jnp.inf))
                    mm = (cb * dec * dt_t[h:h + 1, r0:r0 + SSD_BLOCK]).astype(BF)
                    yd.append(_dot(mm, xs_pair))
                y_ref[r0:r0 + SSD_BLOCK, col:col + LANES] = jnp.where(lane_lo, yd[0], yd[1])
        for q in range(segs_per_blk):
            q0 = r0 + q * seg
            s = q0 // ls
            last_e = cum_e[q0 + seg - 1:q0 + seg, :]
            cum_q = cum_e[q0:q0 + seg, :]
            xs_q = act_ref[q0:q0 + seg, 0:SSM_INNER]
            xw = (xs_q * (jnp.exp(last_e - cum_q) * dt_e[q0:q0 + seg, :])).astype(BF)
            fac_off = jnp.exp(cum_q)
            dec_row = jnp.exp(last_e)
            for g in range(SSM_GROUPS):
                gc0 = g * 4 * SSM_HEADDIM
                gc1 = gc0 + 4 * SSM_HEADDIM
                c_g = cm_bf[q0 - r0:q0 - r0 + seg, g * SSM_STATE:(g + 1) * SSM_STATE]
                b_g = bm_bf[q0 - r0:q0 - r0 + seg, g * SSM_STATE:(g + 1) * SSM_STATE]
                st_g = st_ref[s, :, gc0:gc1]
                y_off = _dot(c_g, st_g.astype(BF)) * fac_off[:, gc0:gc1]
                y_ref[q0:q0 + seg, gc0:gc1] = y_ref[q0:q0 + seg, gc0:gc1] + y_off
                st_ref[s, :, gc0:gc1] = st_g * dec_row[:, gc0:gc1] + _dot_tn(b_g, xw[:, gc0:gc1])

    xs = act_ref[:, 0:SSM_INNER]
    z = _dot(xn, win_ref[:, C_Z:C_Z + SSM_INNER])
    yb = _rms((y_ref[...] + dskip_ref[...] * xs) * _silu(z), normg_ref[...])
    merged = merged + (_sigmoid(_dot(xn, win_ref[:, C_GB:C_GB + D_MODEL]))
                       * _dot(yb.astype(BF), wb_ref[...]))

    m = _dot(merged.astype(BF), wo_ref[...])
    o_ref[...] = (x + _rms(m, post_ref[...])).reshape(nseq, ls, D_MODEL)

    pool_tail = ext_pool[:, ls:ls + POOL_PAD, :]
    conv_tail = ext_conv[:, ls:ls + CONV_PAD, :]
    pool_out[...] = pool_tail
    conv_out[...] = conv_tail
    ext_pool[:, 0:POOL_PAD, :] = pool_tail
    ext_conv[:, 0:CONV_PAD, :] = conv_tail

    @pl.when(t == pl.num_programs(1) - 1)
    def _():
        for s in range(nseq):
            ssm_out[s] = st_ref[s].T


def _mixer_call(x, pool_st, conv_st, ssm_st, pos0, mw, layer, nseq, ls, emit_vn):
    bsz, length, _ = x.shape
    grid = (bsz // nseq, length // ls)
    rows = nseq * ls

    def per_seq(shape):
        return pl.BlockSpec((nseq,) + shape, lambda b, t: (b,) + (0,) * len(shape))

    in_specs = [
        pl.BlockSpec((nseq, ls, D_MODEL), lambda b, t: (b, t, 0)),
        per_seq((POOL_PAD, POOL_W)),
        per_seq((CONV_PAD, XBC_W)),
        per_seq((SSM_INNER, SSM_STATE)),
        _resident((1, D_MODEL), layer),
        _resident((1, D_MODEL), layer),
        _resident((D_MODEL, IN_COLS_R), layer),
        _resident((POOL_W, POOL_W), layer),
        _resident((1, POOL_W), layer),
        _resident((SSM_CONV, XBC_W), layer),
        _resident((1, XBC_W), layer),
        _resident((1, LANES), layer),
        _resident((1, LANES), layer),
        _resident((1, SSM_INNER), layer),
        _resident((1, SSM_INNER), layer),
        _resident((1, GMLP_W), layer),
        _resident((1, GMLP_W), layer),
        _resident((GMLP_HEADS, GMLP_CHUNK, GMLP_CHUNK), layer),
        _resident((GMLP_CHUNK, GMLP_W), layer),
        _resident((POOL_W, D_MODEL), layer),
        _resident((SSM_INNER, D_MODEL), layer),
        _resident((GMLP_W, D_MODEL), layer),
        _resident((D_MODEL, D_MODEL), layer),
    ]
    out_specs = [
        pl.BlockSpec((nseq, ls, D_MODEL), lambda b, t: (b, t, 0)),
        per_seq((POOL_PAD, POOL_W)),
        per_seq((CONV_PAD, XBC_W)),
        per_seq((SSM_INNER, SSM_STATE)),
    ]
    out_shape = [
        jax.ShapeDtypeStruct((bsz, length, D_MODEL), F32),
        jax.ShapeDtypeStruct((bsz, POOL_PAD, POOL_W), F32),
        jax.ShapeDtypeStruct((bsz, CONV_PAD, XBC_W), F32),
        jax.ShapeDtypeStruct((bsz, SSM_INNER, SSM_STATE), F32),
    ]
    if emit_vn:
        out_specs.append(pl.BlockSpec((nseq, ls, GMLP_W), lambda b, t: (b, t, 0)))
        out_shape.append(jax.ShapeDtypeStruct((bsz, length, GMLP_W), F32))
    scratch = [
        pltpu.VMEM((nseq, POOL_PAD + ls, POOL_W), F32),
        pltpu.VMEM((nseq, CONV_PAD + ls, XBC_W), F32),
        pltpu.VMEM((nseq, SSM_STATE, SSM_INNER), F32),
        pltpu.VMEM((rows, XBC_W), F32),
        pltpu.VMEM((rows, SSM_INNER), F32),
    ]
    return pl.pallas_call(
        functools.partial(_mixer_body, nseq, ls, pos0, emit_vn),
        grid=grid,
        in_specs=in_specs,
        out_specs=out_specs,
        out_shape=out_shape,
        scratch_shapes=scratch,
        compiler_params=pltpu.CompilerParams(
            dimension_semantics=("parallel", "arbitrary"), vmem_limit_bytes=VMEM_LIMIT),
        name="mixer_block",
    )(x, pool_st, conv_st, ssm_st, *mw)


def _prep_weights(ffn1_pre_g, ffn1_post_g, ffn1_w_gu, ffn1_w_down,
                  mix_pre_g, mix_post_g, w_in, pool_w, pool_scale,
                  ssm_conv_w, ssm_conv_b, ssm_dt_bias, ssm_a_log, ssm_d, ssm_norm_g,
                  gmlp_norm_g, gmlp_norm_b, gmlp_ws, gmlp_bs,
                  w_branch_a, w_branch_b, w_branch_c, w_out,
                  ffn2_pre_g, ffn2_post_g, ffn2_w_gu, ffn2_w_down):
    row = lambda a: a[:, None, :]
    cuts = np.cumsum([POOL_W, SSM_INNER, XBC_W, SSM_HEADS, GMLP_W, GMLP_W]).tolist()
    xa, z, xbc, dt_c, u, v, gates = jnp.split(w_in, cuts, axis=-1)
    dt_c = jnp.pad(dt_c, ((0, 0), (0, 0), (0, LANES - SSM_HEADS)))
    w_in_r = jnp.concatenate([xa, z, xbc, u, v, gates, dt_c], axis=-1).astype(BF)
    pool_bd = jnp.zeros((DEPTH, POOL_W, POOL_W), F32)
    for g in range(POOL_GROUPS):
        sl = slice(g * POOL_GC, (g + 1) * POOL_GC)
        pool_bd = pool_bd.at[:, sl, sl].set(pool_w[:, g])
    pad_heads = lambda a: row(jnp.pad(a, ((0, 0), (0, LANES - SSM_HEADS))))
    bs_exp = jnp.repeat(jnp.swapaxes(gmlp_bs, 1, 2), GMLP_HC, axis=2)
    ffn1 = (row(ffn1_pre_g), row(ffn1_post_g), ffn1_w_gu.astype(BF), ffn1_w_down.astype(BF))
    ffn2 = (row(ffn2_pre_g), row(ffn2_post_g), ffn2_w_gu.astype(BF), ffn2_w_down.astype(BF))
    mix = (row(mix_pre_g), row(mix_post_g), w_in_r, pool_bd.astype(BF), row(pool_scale),
           ssm_conv_w, row(ssm_conv_b), pad_heads(ssm_dt_bias), pad_heads(ssm_a_log),
           row(jnp.repeat(ssm_d, SSM_HEADDIM, axis=1)), row(ssm_norm_g),
           row(gmlp_norm_g), row(gmlp_norm_b), gmlp_ws.astype(BF), bs_exp,
           w_branch_a.astype(BF), w_branch_b.astype(BF), w_branch_c.astype(BF), w_out.astype(BF))
    return ffn1, mix, ffn2


def _run_trunk(x, pool_st, conv_st, ssm_st, pos0, ffn1, mix, ffn2, tm, nseq, ls, emit_vn):
    bsz, length, _ = x.shape
    depth = pool_st.shape[0]
    pool_p = jnp.pad(pool_st, ((0, 0), (0, 0), (POOL_PAD - POOL_HIST, 0), (0, 0)))
    conv_p = jnp.pad(conv_st, ((0, 0), (0, 0), (CONV_PAD - (SSM_CONV - 1), 0), (0, 0)))
    ssm_p = ssm_st.reshape(depth, bsz, SSM_INNER, SSM_STATE)
    new_pool, new_conv, new_ssm, new_v = [], [], [], []
    h = x
    for l in range(depth):
        h = _ffn_call(h.reshape(bsz * length, D_MODEL), *ffn1, l, tm).reshape(bsz, length, D_MODEL)
        outs = _mixer_call(h, pool_p[l], conv_p[l], ssm_p[l], pos0, mix, l, nseq, ls, emit_vn)
        h = outs[0]
        new_pool.append(outs[1][:, POOL_PAD - POOL_HIST:])
        new_conv.append(outs[2][:, CONV_PAD - (SSM_CONV - 1):])
        new_ssm.append(outs[3].reshape(bsz, SSM_HEADS, SSM_HEADDIM, SSM_STATE))
        if emit_vn:
            new_v.append(outs[4])
        h = _ffn_call(h.reshape(bsz * length, D_MODEL), *ffn2, l, tm).reshape(bsz, length, D_MODEL)
    return h, jnp.stack(new_pool), jnp.stack(new_conv), jnp.stack(new_ssm), new_v


def kernel(x_prompt, x_sample, state_pool, state_conv, state_ssm, ffn1_pre_g, ffn1_post_g, ffn1_w_gu, ffn1_w_down, mix_pre_g, mix_post_g, w_in, pool_w, pool_scale, ssm_conv_w, ssm_conv_b, ssm_dt_bias, ssm_a_log, ssm_d, ssm_norm_g, gmlp_norm_g, gmlp_norm_b, gmlp_ws, gmlp_bs, w_branch_a, w_branch_b, w_branch_c, w_out, ffn2_pre_g, ffn2_post_g, ffn2_w_gu, ffn2_w_down):
    ffn1, mix, ffn2 = _prep_weights(
        ffn1_pre_g, ffn1_post_g, ffn1_w_gu, ffn1_w_down, mix_pre_g, mix_post_g, w_in, pool_w,
        pool_scale, ssm_conv_w, ssm_conv_b, ssm_dt_bias, ssm_a_log, ssm_d, ssm_norm_g,
        gmlp_norm_g, gmlp_norm_b, gmlp_ws, gmlp_bs, w_branch_a, w_branch_b, w_branch_c, w_out,
        ffn2_pre_g, ffn2_post_g, ffn2_w_gu, ffn2_w_down)
    bp = x_prompt.shape[0]
    bs, ls = x_sample.shape[0], x_sample.shape[1]
    zero_pool = jnp.zeros((DEPTH, bp, POOL_HIST, POOL_W), F32)
    zero_conv = jnp.zeros((DEPTH, bp, SSM_CONV - 1, XBC_W), F32)
    zero_ssm = jnp.zeros((DEPTH, bp, SSM_HEADS, SSM_HEADDIM, SSM_STATE), F32)
    y_p, pool_p, conv_p, ssm_p, _ = _run_trunk(
        x_prompt, zero_pool, zero_conv, zero_ssm, 0, ffn1, mix, ffn2,
        tm=512, nseq=1, ls=256, emit_vn=False)
    y_s, pool_s, conv_s, ssm_s, v_rows = _run_trunk(
        x_sample, state_pool, state_conv, state_ssm, PAST_LEN, ffn1, mix, ffn2,
        tm=bs * ls, nseq=bs, ls=ls, emit_vn=True)
    return (y_p, y_s, pool_p, conv_p, ssm_p, pool_s, conv_s, ssm_s, jnp.stack(v_rows))
```

```python
import functools

import numpy as np
import jax
import jax.numpy as jnp
from jax import lax
from jax.experimental import pallas as pl
from jax.experimental.pallas import tpu as pltpu

D_MODEL = 1024
DEPTH = 4
D_FF = 2816
EPS = 1e-6
POOL_W = 512
POOL_GROUPS = 4
POOL_GC = 128
POOL_WINDOWS = (2, 4, 8, 16)
POOL_HIST = 15
SSM_INNER = 1024
SSM_HEADDIM = 64
SSM_HEADS = 16
SSM_GROUPS = 4
SSM_STATE = 128
SSM_CONV = 4
XBC_W = 2048
GMLP_W = 512
GMLP_HEADS = 4
GMLP_HC = 128
GMLP_CHUNK = 128
PAST_LEN = 4096

LANES = 128
SUBLANES = 8
POOL_PAD = 16
CONV_PAD = 8
SSD_BLOCK = 128
FF_COLS = 256
VMEM_LIMIT = 60 * 1024 * 1024
PROMPT_FFN_ROWS = 1024
PROMPT_MIX_SEQS = 1
PROMPT_MIX_ROWS = 512

C_XA = 0
C_Z = C_XA + POOL_W
C_XBC = C_Z + SSM_INNER
C_U = C_XBC + XBC_W
C_V = C_U + GMLP_W
C_GA = C_V + GMLP_W
C_GB = C_GA + D_MODEL
C_GC = C_GB + D_MODEL
C_DT = C_GC + D_MODEL
IN_COLS_R = C_DT + LANES

BF = jnp.bfloat16
F32 = jnp.float32


def _dot(a, b):
    return jnp.dot(a, b, preferred_element_type=F32)


def _dot_nt(a, b):
    return lax.dot_general(a, b, (((1,), (1,)), ((), ())), preferred_element_type=F32)


def _dot_tn(a, b):
    return lax.dot_general(a, b, (((0,), (0,)), ((), ())), preferred_element_type=F32)


def _rms(x, g):
    return x * lax.rsqrt(jnp.mean(x * x, axis=-1, keepdims=True) + EPS) * g


def _sigmoid(x):
    return 0.5 * jnp.tanh(0.5 * x) + 0.5


def _silu(x):
    hx = 0.5 * x
    return hx * jnp.tanh(hx) + hx


def _softplus(x):
    return jnp.maximum(x, 0.0) + jnp.log1p(jnp.exp(-jnp.abs(x)))


def _div_pow2(i, d):
    assert d & (d - 1) == 0
    return jnp.right_shift(i, d.bit_length() - 1)


def _split_bf16(x, parts):
    out = []
    for _ in range(parts - 1):
        p = x.astype(BF)
        out.append(p)
        x = x - p.astype(F32)
    out.append(x.astype(BF))
    return out


def _ffn_body(x_ref, pre_ref, post_ref, wgu_ref, wd_ref, o_ref, h_ref):
    x = x_ref[...]
    xn = _rms(x, pre_ref[...]).astype(BF)
    for j in range(D_FF // FF_COLS):
        g = _dot(xn, wgu_ref[:, j * FF_COLS:(j + 1) * FF_COLS])
        u = _dot(xn, wgu_ref[:, D_FF + j * FF_COLS:D_FF + (j + 1) * FF_COLS])
        h_ref[:, j * FF_COLS:(j + 1) * FF_COLS] = (_silu(g) * u).astype(BF)
    f = _dot(h_ref[...], wd_ref[...])
    o_ref[...] = x + 0.5 * _rms(f, post_ref[...])


def _resident(shape, layer):
    nd = len(shape)
    return pl.BlockSpec((None,) + tuple(shape), lambda *_: (layer,) + (0,) * nd,
                        pipeline_mode=pl.Buffered(1))


def _ffn_call(x2d, pre_g, post_g, w_gu, w_down, layer, tm):
    n = x2d.shape[0]
    return pl.pallas_call(
        _ffn_body,
        grid=(n // tm,),
        in_specs=[
            pl.BlockSpec((tm, D_MODEL), lambda i: (i, 0)),
            _resident((1, D_MODEL), layer),
            _resident((1, D_MODEL), layer),
            _resident((D_MODEL, 2 * D_FF), layer),
            _resident((D_FF, D_MODEL), layer),
        ],
        out_specs=pl.BlockSpec((tm, D_MODEL), lambda i: (i, 0)),
        out_shape=jax.ShapeDtypeStruct((n, D_MODEL), F32),
        scratch_shapes=[pltpu.VMEM((tm, D_FF), BF)],
        compiler_params=pltpu.CompilerParams(
            dimension_semantics=("parallel",), vmem_limit_bytes=VMEM_LIMIT),
        name="swiglu_block",
    )(x2d, pre_g, post_g, w_gu, w_down)


def _mixer_body(nseq, ls, pos0, emit_vn,
                x_ref, pool_in, conv_in, ssm_in, pre_ref, post_ref, win_ref,
                poolw_ref, pscale_ref, convw_ref, convb_ref, dtb_ref, alog_ref, dskip_ref,
                normg_ref, gng_ref, gnb_ref, ws_ref, bs_ref, wa_ref, wb_ref, wc_ref, wo_ref,
                *rest):
    if emit_vn:
        o_ref, pool_out, conv_out, ssm_out, vn_out = rest[:5]
        ext_pool, ext_conv, st_ref, act_ref, y_ref = rest[5:]
    else:
        o_ref, pool_out, conv_out, ssm_out = rest[:4]
        vn_out = None
        ext_pool, ext_conv, st_ref, act_ref, y_ref = rest[4:]

    rows = nseq * ls
    t = pl.program_id(1)
    seg = min(ls, SSD_BLOCK)

    @pl.when(t == 0)
    def _():
        ext_pool[:, 0:POOL_PAD, :] = pool_in[...]
        ext_conv[:, 0:CONV_PAD, :] = conv_in[...]
        for s in range(nseq):
            st_ref[s] = ssm_in[s].T

    x = x_ref[...].reshape(rows, D_MODEL)
    xn = _rms(x, pre_ref[...]).astype(BF)

    xa = _dot(xn, win_ref[:, C_XA:C_XA + POOL_W])
    for s in range(nseq):
        ext_pool[s, POOL_PAD:POOL_PAD + ls, :] = xa[s * ls:(s + 1) * ls]
    pos = pos0 + t * ls + lax.broadcasted_iota(jnp.int32, (ls, 1), 0)
    zz_rows = []
    for s in range(nseq):
        zz_g = []
        for gi, w in enumerate(POOL_WINDOWS):
            c0 = gi * POOL_GC
            acc = ext_pool[s, POOL_PAD:POOL_PAD + ls, c0:c0 + POOL_GC]
            for k in range(1, w):
                acc = acc + ext_pool[s, POOL_PAD - k:POOL_PAD - k + ls, c0:c0 + POOL_GC]
            inv = 1.0 / jnp.minimum(pos + 1, w).astype(F32)
            zz_g.append(acc * inv - ext_pool[s, POOL_PAD:POOL_PAD + ls, c0:c0 + POOL_GC])
        zz_rows.append(jnp.concatenate(zz_g, axis=1))
    zz = zz_rows[0] if nseq == 1 else jnp.concatenate(zz_rows, axis=0)
    ya = _dot(zz.astype(BF), poolw_ref[...]) * pscale_ref[...]
    merged = _sigmoid(_dot(xn, win_ref[:, C_GA:C_GA + D_MODEL])) * _dot(ya.astype(BF), wa_ref[...])

    u = _dot(xn, win_ref[:, C_U:C_U + GMLP_W])
    v = _dot(xn, win_ref[:, C_V:C_V + GMLP_W])
    mu = jnp.mean(v, axis=-1, keepdims=True)
    vc = v - mu
    vn = vc * lax.rsqrt(jnp.mean(vc * vc, axis=-1, keepdims=True) + EPS) * gng_ref[...] + gnb_ref[...]
    if emit_vn:
        vn_out[...] = vn.reshape(nseq, ls, GMLP_W)
    vn_bf = vn.astype(BF)
    gq = min(ls, GMLP_CHUNK)
    tri_g = (lax.broadcasted_iota(jnp.int32, (gq, gq), 0)
             >= lax.broadcasted_iota(jnp.int32, (gq, gq), 1))
    wsm = [jnp.where(tri_g, ws_ref[h, 0:gq, 0:gq], jnp.zeros((), BF)) for h in range(GMLP_HEADS)]
    bias_g = bs_ref[0:gq, :]
    sv_rows = []
    for c in range(rows // gq):
        sv_h = [_dot(wsm[h], vn_bf[c * gq:(c + 1) * gq, h * GMLP_HC:(h + 1) * GMLP_HC])
                for h in range(GMLP_HEADS)]
        sv_rows.append(jnp.concatenate(sv_h, axis=1) + bias_g)
    sv = sv_rows[0] if len(sv_rows) == 1 else jnp.concatenate(sv_rows, axis=0)
    yc = u * sv
    merged = merged + (_sigmoid(_dot(xn, win_ref[:, C_GC:C_GC + D_MODEL]))
                       * _dot(yc.astype(BF), wc_ref[...]))

    xbc_pre = _dot(xn, win_ref[:, C_XBC:C_XBC + XBC_W])
    for s in range(nseq):
        ext_conv[s, CONV_PAD:CONV_PAD + ls, :] = xbc_pre[s * ls:(s + 1) * ls]
    for s in range(nseq):
        base = CONV_PAD - (SSM_CONV - 1)
        cv = convb_ref[...] + convw_ref[0:1, :] * ext_conv[s, base:base + ls, :]
        for k in range(1, SSM_CONV):
            cv = cv + convw_ref[k:k + 1, :] * ext_conv[s, base + k:base + k + ls, :]
        act_ref[s * ls:(s + 1) * ls, :] = _silu(cv)

    dt = _softplus(_dot(xn, win_ref[:, C_DT:C_DT + LANES]) + dtb_ref[...])
    d_a = dt * (-jnp.exp(alog_ref[...]))
    ri = lax.broadcasted_iota(jnp.int32, (rows, rows), 0)
    ci = lax.broadcasted_iota(jnp.int32, (rows, rows), 1)
    seg_tri = jnp.where((_div_pow2(ri, seg) == _div_pow2(ci, seg)) & (ri >= ci), 1.0, 0.0).astype(BF)
    cum = None
    for p in _split_bf16(d_a, 3):
        part = _dot(seg_tri, p)
        cum = part if cum is None else cum + part
    ei = lax.broadcasted_iota(jnp.int32, (LANES, SSM_INNER), 0)
    ej = lax.broadcasted_iota(jnp.int32, (LANES, SSM_INNER), 1)
    expand = jnp.where(_div_pow2(ej, SSM_HEADDIM) == ei, 1.0, 0.0).astype(BF)

    def _expand(a, parts):
        out = None
        for p in _split_bf16(a, parts):
            e = _dot(p, expand)
            out = e if out is None else out + e
        return out

    nseg = rows // seg
    lasts = [cum[(q + 1) * seg - 1:(q + 1) * seg, :] for q in range(nseg)]
    fxw = [jnp.exp(lasts[q] - cum[q * seg:(q + 1) * seg, :]) * dt[q * seg:(q + 1) * seg, :]
           for q in range(nseg)]
    fxw = fxw[0] if nseg == 1 else jnp.concatenate(fxw, axis=0)
    fac_xw_e = _expand(fxw, 1)
    fac_off_e = _expand(jnp.exp(cum), 1)
    dec_e = [_expand(jnp.broadcast_to(jnp.exp(lasts[q]), (SUBLANES, LANES)), 3)[0:1, :]
             for q in range(nseg)]
    cum_t = cum.T
    dt_t = dt.T

    bi = lax.broadcasted_iota(jnp.int32, (SSD_BLOCK, SSD_BLOCK), 0)
    bj = lax.broadcasted_iota(jnp.int32, (SSD_BLOCK, SSD_BLOCK), 1)
    blk_tri = (_div_pow2(bi, seg) == _div_pow2(bj, seg)) & (bi >= bj)
    lane_lo = lax.broadcasted_iota(jnp.int32, (SSD_BLOCK, LANES), 1) < SSM_HEADDIM
    nblk = rows // SSD_BLOCK
    segs_per_blk = SSD_BLOCK // seg
    for b in range(nblk):
        r0 = b * SSD_BLOCK
        xs_bf = act_ref[r0:r0 + SSD_BLOCK, 0:SSM_INNER].astype(BF)
        bm_bf = act_ref[r0:r0 + SSD_BLOCK, SSM_INNER:SSM_INNER + SSM_GROUPS * SSM_STATE].astype(BF)
        cm_bf = act_ref[r0:r0 + SSD_BLOCK, SSM_INNER + SSM_GROUPS * SSM_STATE:XBC_W].astype(BF)
        for g in range(SSM_GROUPS):
            c_g = cm_bf[:, g * SSM_STATE:(g + 1) * SSM_STATE]
            b_g = bm_bf[:, g * SSM_STATE:(g + 1) * SSM_STATE]
            cb = jnp.where(blk_tri, _dot_nt(c_g, b_g), 0.0)
            for j in range(2):
                col = (2 * g + j) * LANES
                xs_pair = xs_bf[:, col:col + LANES]
                yd = []
                for hh in range(2):
                    h = 4 * g + 2 * j + hh
                    diff = cum[r0:r0 + SSD_BLOCK, h:h + 1] - cum_t[h:h + 1, r0:r0 + SSD_BLOCK]
                    dec = jnp.exp(jnp.where(blk_tri, diff, -jnp.inf))
                    mm = (cb * dec * dt_t[h:h + 1, r0:r0 + SSD_BLOCK]).astype(BF)
                    yd.append(_dot(mm, xs_pair))
                y_ref[r0:r0 + SSD_BLOCK, col:col + LANES] = jnp.where(lane_lo, yd[0], yd[1])
        for q in range(segs_per_blk):
            q0 = r0 + q * seg
            s = q0 // ls
            xs_q = act_ref[q0:q0 + seg, 0:SSM_INNER]
            xw = (xs_q * fac_xw_e[q0:q0 + seg, :]).astype(BF)
            fac_off = fac_off_e[q0:q0 + seg, :]
            dec_row = dec_e[q0 // seg]
            for g in range(SSM_GROUPS):
                gc0 = g * 4 * SSM_HEADDIM
                gc1 = gc0 + 4 * SSM_HEADDIM
                c_g = cm_bf[q0 - r0:q0 - r0 + seg, g * SSM_STATE:(g + 1) * SSM_STATE]
                b_g = bm_bf[q0 - r0:q0 - r0 + seg, g * SSM_STATE:(g + 1) * SSM_STATE]
                st_g = st_ref[s, :, gc0:gc1]
                y_off = _dot(c_g, st_g.astype(BF)) * fac_off[:, gc0:gc1]
                y_ref[q0:q0 + seg, gc0:gc1] = y_ref[q0:q0 + seg, gc0:gc1] + y_off
                st_ref[s, :, gc0:gc1] = st_g * dec_row[:, gc0:gc1] + _dot_tn(b_g, xw[:, gc0:gc1])

    xs = act_ref[:, 0:SSM_INNER]
    z = _dot(xn, win_ref[:, C_Z:C_Z + SSM_INNER])
    yb = _rms((y_ref[...] + dskip_ref[...] * xs) * _silu(z), normg_ref[...])
    merged = merged + (_sigmoid(_dot(xn, win_ref[:, C_GB:C_GB + D_MODEL]))
                       * _dot(yb.astype(BF), wb_ref[...]))

    m = _dot(merged.astype(BF), wo_ref[...])
    o_ref[...] = (x + _rms(m, post_ref[...])).reshape(nseq, ls, D_MODEL)

    pool_tail = ext_pool[:, ls:ls + POOL_PAD, :]
    conv_tail = ext_conv[:, ls:ls + CONV_PAD, :]
    pool_out[...] = pool_tail
    conv_out[...] = conv_tail
    ext_pool[:, 0:POOL_PAD, :] = pool_tail
    ext_conv[:, 0:CONV_PAD, :] = conv_tail

    @pl.when(t == pl.num_programs(1) - 1)
    def _():
        for s in range(nseq):
            ssm_out[s] = st_ref[s].T


def _mixer_call(x, pool_st, conv_st, ssm_st, pos0, mw, layer, nseq, ls, emit_vn):
    bsz, length, _ = x.shape
    grid = (bsz // nseq, length // ls)
    rows = nseq * ls

    def per_seq(shape):
        return pl.BlockSpec((nseq,) + shape, lambda b, t: (b,) + (0,) * len(shape))

    in_specs = [
        pl.BlockSpec((nseq, ls, D_MODEL), lambda b, t: (b, t, 0)),
        per_seq((POOL_PAD, POOL_W)),
        per_seq((CONV_PAD, XBC_W)),
        per_seq((SSM_INNER, SSM_STATE)),
        _resident((1, D_MODEL), layer),
        _resident((1, D_MODEL), layer),
        _resident((D_MODEL, IN_COLS_R), layer),
        _resident((POOL_W, POOL_W), layer),
        _resident((1, POOL_W), layer),
        _resident((SSM_CONV, XBC_W), layer),
        _resident((1, XBC_W), layer),
        _resident((1, LANES), layer),
        _resident((1, LANES), layer),
        _resident((1, SSM_INNER), layer),
        _resident((1, SSM_INNER), layer),
        _resident((1, GMLP_W), layer),
        _resident((1, GMLP_W), layer),
        _resident((GMLP_HEADS, GMLP_CHUNK, GMLP_CHUNK), layer),
        _resident((GMLP_CHUNK, GMLP_W), layer),
        _resident((POOL_W, D_MODEL), layer),
        _resident((SSM_INNER, D_MODEL), layer),
        _resident((GMLP_W, D_MODEL), layer),
        _resident((D_MODEL, D_MODEL), layer),
    ]
    out_specs = [
        pl.BlockSpec((nseq, ls, D_MODEL), lambda b, t: (b, t, 0)),
        per_seq((POOL_PAD, POOL_W)),
        per_seq((CONV_PAD, XBC_W)),
        per_seq((SSM_INNER, SSM_STATE)),
    ]
    out_shape = [
        jax.ShapeDtypeStruct((bsz, length, D_MODEL), F32),
        jax.ShapeDtypeStruct((bsz, POOL_PAD, POOL_W), F32),
        jax.ShapeDtypeStruct((bsz, CONV_PAD, XBC_W), F32),
        jax.ShapeDtypeStruct((bsz, SSM_INNER, SSM_STATE), F32),
    ]
    if emit_vn:
        out_specs.append(pl.BlockSpec((nseq, ls, GMLP_W), lambda b, t: (b, t, 0)))
        out_shape.append(jax.ShapeDtypeStruct((bsz, length, GMLP_W), F32))
    scratch = [
        pltpu.VMEM((nseq, POOL_PAD + ls, POOL_W), F32),
        pltpu.VMEM((nseq, CONV_PAD + ls, XBC_W), F32),
        pltpu.VMEM((nseq, SSM_STATE, SSM_INNER), F32),
        pltpu.VMEM((rows, XBC_W), F32),
        pltpu.VMEM((rows, SSM_INNER), F32),
    ]
    return pl.pallas_call(
        functools.partial(_mixer_body, nseq, ls, pos0, emit_vn),
        grid=grid,
        in_specs=in_specs,
        out_specs=out_specs,
        out_shape=out_shape,
        scratch_shapes=scratch,
        compiler_params=pltpu.CompilerParams(
            dimension_semantics=("parallel", "arbitrary"), vmem_limit_bytes=VMEM_LIMIT),
        name="mixer_block",
    )(x, pool_st, conv_st, ssm_st, *mw)


def _prep_weights(ffn1_pre_g, ffn1_post_g, ffn1_w_gu, ffn1_w_down,
                  mix_pre_g, mix_post_g, w_in, pool_w, pool_scale,
                  ssm_conv_w, ssm_conv_b, ssm_dt_bias, ssm_a_log, ssm_d, ssm_norm_g,
                  gmlp_norm_g, gmlp_norm_b, gmlp_ws, gmlp_bs,
                  w_branch_a, w_branch_b, w_branch_c, w_out,
                  ffn2_pre_g, ffn2_post_g, ffn2_w_gu, ffn2_w_down):
    row = lambda a: a[:, None, :]
    cuts = np.cumsum([POOL_W, SSM_INNER, XBC_W, SSM_HEADS, GMLP_W, GMLP_W]).tolist()
    xa, z, xbc, dt_c, u, v, gates = jnp.split(w_in, cuts, axis=-1)
    dt_c = jnp.pad(dt_c, ((0, 0), (0, 0), (0, LANES - SSM_HEADS)))
    w_in_r = jnp.concatenate([xa, z, xbc, u, v, gates, dt_c], axis=-1).astype(BF)
    pool_bd = jnp.zeros((DEPTH, POOL_W, POOL_W), F32)
    for g in range(POOL_GROUPS):
        sl = slice(g * POOL_GC, (g + 1) * POOL_GC)
        pool_bd = pool_bd.at[:, sl, sl].set(pool_w[:, g])
    pad_heads = lambda a: row(jnp.pad(a, ((0, 0), (0, LANES - SSM_HEADS))))
    bs_exp = jnp.repeat(jnp.swapaxes(gmlp_bs, 1, 2), GMLP_HC, axis=2)
    ffn1 = (row(ffn1_pre_g), row(ffn1_post_g), ffn1_w_gu.astype(BF), ffn1_w_down.astype(BF))
    ffn2 = (row(ffn2_pre_g), row(ffn2_post_g), ffn2_w_gu.astype(BF), ffn2_w_down.astype(BF))
    mix = (row(mix_pre_g), row(mix_post_g), w_in_r, pool_bd.astype(BF), row(pool_scale),
           ssm_conv_w, row(ssm_conv_b), pad_heads(ssm_dt_bias), pad_heads(ssm_a_log),
           row(jnp.repeat(ssm_d, SSM_HEADDIM, axis=1)), row(ssm_norm_g),
           row(gmlp_norm_g), row(gmlp_norm_b), gmlp_ws.astype(BF), bs_exp,
           w_branch_a.astype(BF), w_branch_b.astype(BF), w_branch_c.astype(BF), w_out.astype(BF))
    return ffn1, mix, ffn2


def _run_trunk(x, pool_st, conv_st, ssm_st, pos0, ffn1, mix, ffn2, tm, nseq, ls, emit_vn):
    bsz, length, _ = x.shape
    depth = pool_st.shape[0]
    pool_p = jnp.pad(pool_st, ((0, 0), (0, 0), (POOL_PAD - POOL_HIST, 0), (0, 0)))
    conv_p = jnp.pad(conv_st, ((0, 0), (0, 0), (CONV_PAD - (SSM_CONV - 1), 0), (0, 0)))
    ssm_p = ssm_st.reshape(depth, bsz, SSM_INNER, SSM_STATE)
    new_pool, new_conv, new_ssm, new_v = [], [], [], []
    h = x
    for l in range(depth):
        h = _ffn_call(h.reshape(bsz * length, D_MODEL), *ffn1, l, tm).reshape(bsz, length, D_MODEL)
        outs = _mixer_call(h, pool_p[l], conv_p[l], ssm_p[l], pos0, mix, l, nseq, ls, emit_vn)
        h = outs[0]
        new_pool.append(outs[1][:, POOL_PAD - POOL_HIST:])
        new_conv.append(outs[2][:, CONV_PAD - (SSM_CONV - 1):])
        new_ssm.append(outs[3].reshape(bsz, SSM_HEADS, SSM_HEADDIM, SSM_STATE))
        if emit_vn:
            new_v.append(outs[4])
        h = _ffn_call(h.reshape(bsz * length, D_MODEL), *ffn2, l, tm).reshape(bsz, length, D_MODEL)
    return h, jnp.stack(new_pool), jnp.stack(new_conv), jnp.stack(new_ssm), new_v


def kernel(x_prompt, x_sample, state_pool, state_conv, state_ssm, ffn1_pre_g, ffn1_post_g, ffn1_w_gu, ffn1_w_down, mix_pre_g, mix_post_g, w_in, pool_w, pool_scale, ssm_conv_w, ssm_conv_b, ssm_dt_bias, ssm_a_log, ssm_d, ssm_norm_g, gmlp_norm_g, gmlp_norm_b, gmlp_ws, gmlp_bs, w_branch_a, w_branch_b, w_branch_c, w_out, ffn2_pre_g, ffn2_post_g, ffn2_w_gu, ffn2_w_down):
    ffn1, mix, ffn2 = _prep_weights(
        ffn1_pre_g, ffn1_post_g, ffn1_w_gu, ffn1_w_down, mix_pre_g, mix_post_g, w_in, pool_w,
        pool_scale, ssm_conv_w, ssm_conv_b, ssm_dt_bias, ssm_a_log, ssm_d, ssm_norm_g,
        gmlp_norm_g, gmlp_norm_b, gmlp_ws, gmlp_bs, w_branch_a, w_branch_b, w_branch_c, w_out,
        ffn2_pre_g, ffn2_post_g, ffn2_w_gu, ffn2_w_down)
    bp = x_prompt.shape[0]
    bs, ls = x_sample.shape[0], x_sample.shape[1]
    zero_pool = jnp.zeros((DEPTH, bp, POOL_HIST, POOL_W), F32)
    zero_conv = jnp.zeros((DEPTH, bp, SSM_CONV - 1, XBC_W), F32)
    zero_ssm = jnp.zeros((DEPTH, bp, SSM_HEADS, SSM_HEADDIM, SSM_STATE), F32)
    y_p, pool_p, conv_p, ssm_p, _ = _run_trunk(
        x_prompt, zero_pool, zero_conv, zero_ssm, 0, ffn1, mix, ffn2,
        tm=PROMPT_FFN_ROWS, nseq=PROMPT_MIX_SEQS, ls=PROMPT_MIX_ROWS, emit_vn=False)
    y_s, pool_s, conv_s, ssm_s, v_rows = _run_trunk(
        x_sample, state_pool, state_conv, state_ssm, PAST_LEN, ffn1, mix, ffn2,
        tm=bs * ls, nseq=bs, ls=ls, emit_vn=True)
    return (y_p, y_s, pool_p, conv_p, ssm_p, pool_s, conv_s, ssm_s, jnp.stack(v_rows))
```

```python
import functools

import numpy as np
import jax
import jax.numpy as jnp
from jax import lax
from jax.experimental import pallas as pl
from jax.experimental.pallas import tpu as pltpu

D_MODEL = 1024
DEPTH = 4
D_FF = 2816
EPS = 1e-6
POOL_W = 512
POOL_GROUPS = 4
POOL_GC = 128
POOL_WINDOWS = (2, 4, 8, 16)
POOL_HIST = 15
SSM_INNER = 1024
SSM_HEADDIM = 64
SSM_HEADS = 16
SSM_GROUPS = 4
SSM_STATE = 128
SSM_CONV = 4
XBC_W = 2048
GMLP_W = 512
GMLP_HEADS = 4
GMLP_HC = 128
GMLP_CHUNK = 128
PAST_LEN = 4096

LANES = 128
SUBLANES = 8
POOL_PAD = 16
CONV_PAD = 8
SSD_BLOCK = 128
FF_COLS = 256
VMEM_LIMIT = 60 * 1024 * 1024
PROMPT_FFN_ROWS = 1024
PROMPT_MIX_SEQS = 1
PROMPT_MIX_ROWS = 512

C_XA = 0
C_Z = C_XA + POOL_W
C_XBC = C_Z + SSM_INNER
C_U = C_XBC + XBC_W
C_V = C_U + GMLP_W
C_GA = C_V + GMLP_W
C_GB = C_GA + D_MODEL
C_GC = C_GB + D_MODEL
C_DT = C_GC + D_MODEL
IN_COLS_R = C_DT + LANES

BF = jnp.bfloat16
F32 = jnp.float32


def _dot(a, b):
    return jnp.dot(a, b, preferred_element_type=F32)


def _dot_nt(a, b):
    return lax.dot_general(a, b, (((1,), (1,)), ((), ())), preferred_element_type=F32)


def _dot_tn(a, b):
    return lax.dot_general(a, b, (((0,), (0,)), ((), ())), preferred_element_type=F32)


def _rms(x, g):
    return x * lax.rsqrt(jnp.mean(x * x, axis=-1, keepdims=True) + EPS) * g


def _sigmoid(x):
    return 0.5 * jnp.tanh(0.5 * x) + 0.5


def _silu(x):
    hx = 0.5 * x
    return hx * jnp.tanh(hx) + hx


def _softplus(x):
    return jnp.maximum(x, 0.0) + jnp.log1p(jnp.exp(-jnp.abs(x)))


def _div_pow2(i, d):
    assert d & (d - 1) == 0
    return jnp.right_shift(i, d.bit_length() - 1)


def _split_bf16(x, parts):
    out = []
    for _ in range(parts - 1):
        p = x.astype(BF)
        out.append(p)
        x = x - p.astype(F32)
    out.append(x.astype(BF))
    return out


def _ffn_body(x_ref, pre_ref, post_ref, wgu_ref, wd_ref, o_ref, h_ref):
    x = x_ref[...]
    xn = _rms(x, pre_ref[...]).astype(BF)
    for j in range(D_FF // FF_COLS):
        g = _dot(xn, wgu_ref[:, j * FF_COLS:(j + 1) * FF_COLS])
        u = _dot(xn, wgu_ref[:, D_FF + j * FF_COLS:D_FF + (j + 1) * FF_COLS])
        h_ref[:, j * FF_COLS:(j + 1) * FF_COLS] = (_silu(g) * u).astype(BF)
    f = _dot(h_ref[...], wd_ref[...])
    o_ref[...] = x + 0.5 * _rms(f, post_ref[...])


def _resident(shape, layer):
    nd = len(shape)
    return pl.BlockSpec((None,) + tuple(shape), lambda *_: (layer,) + (0,) * nd,
                        pipeline_mode=pl.Buffered(1))


def _ffn_call(x2d, pre_g, post_g, w_gu, w_down, layer, tm):
    n = x2d.shape[0]
    return pl.pallas_call(
        _ffn_body,
        grid=(n // tm,),
        in_specs=[
            pl.BlockSpec((tm, D_MODEL), lambda i: (i, 0)),
            _resident((1, D_MODEL), layer),
            _resident((1, D_MODEL), layer),
            _resident((D_MODEL, 2 * D_FF), layer),
            _resident((D_FF, D_MODEL), layer),
        ],
        out_specs=pl.BlockSpec((tm, D_MODEL), lambda i: (i, 0)),
        out_shape=jax.ShapeDtypeStruct((n, D_MODEL), F32),
        scratch_shapes=[pltpu.VMEM((tm, D_FF), BF)],
        compiler_params=pltpu.CompilerParams(
            dimension_semantics=("parallel",), vmem_limit_bytes=VMEM_LIMIT),
        name="swiglu_block",
    )(x2d, pre_g, post_g, w_gu, w_down)


def _mixer_body(nseq, ls, pos0, emit_vn,
                x_ref, pool_in, conv_in, ssm_in, pre_ref, post_ref, win_ref,
                poolw_ref, pscale_ref, convw_ref, convb_ref, dtb_ref, alog_ref, dskip_ref,
                normg_ref, gng_ref, gnb_ref, ws_ref, bs_ref, wa_ref, wb_ref, wc_ref, wo_ref,
                *rest):
    if emit_vn:
        o_ref, pool_out, conv_out, ssm_out, vn_out = rest[:5]
        ext_pool, ext_conv, st_ref, act_ref, y_ref = rest[5:]
    else:
        o_ref, pool_out, conv_out, ssm_out = rest[:4]
        vn_out = None
        ext_pool, ext_conv, st_ref, act_ref, y_ref = rest[4:]

    rows = nseq * ls
    t = pl.program_id(1)
    seg = min(ls, SSD_BLOCK)

    @pl.when(t == 0)
    def _():
        ext_pool[:, 0:POOL_PAD, :] = pool_in[...]
        ext_conv[:, 0:CONV_PAD, :] = conv_in[...]
        for s in range(nseq):
            st_ref[s] = ssm_in[s].T

    x = x_ref[...].reshape(rows, D_MODEL)
    xn = _rms(x, pre_ref[...]).astype(BF)

    xa = _dot(xn, win_ref[:, C_XA:C_XA + POOL_W])
    for s in range(nseq):
        ext_pool[s, POOL_PAD:POOL_PAD + ls, :] = xa[s * ls:(s + 1) * ls]
    pos = pos0 + t * ls + lax.broadcasted_iota(jnp.int32, (ls, 1), 0)
    zz_rows = []
    for s in range(nseq):
        zz_g = []
        for gi, w in enumerate(POOL_WINDOWS):
            c0 = gi * POOL_GC
            acc = ext_pool[s, POOL_PAD:POOL_PAD + ls, c0:c0 + POOL_GC]
            for k in range(1, w):
                acc = acc + ext_pool[s, POOL_PAD - k:POOL_PAD - k + ls, c0:c0 + POOL_GC]
            inv = 1.0 / jnp.minimum(pos + 1, w).astype(F32)
            zz_g.append(acc * inv - ext_pool[s, POOL_PAD:POOL_PAD + ls, c0:c0 + POOL_GC])
        zz_rows.append(jnp.concatenate(zz_g, axis=1))
    zz = zz_rows[0] if nseq == 1 else jnp.concatenate(zz_rows, axis=0)
    ya = _dot(zz.astype(BF), poolw_ref[...]) * pscale_ref[...]
    merged = _sigmoid(_dot(xn, win_ref[:, C_GA:C_GA + D_MODEL])) * _dot(ya.astype(BF), wa_ref[...])

    u = _dot(xn, win_ref[:, C_U:C_U + GMLP_W])
    v = _dot(xn, win_ref[:, C_V:C_V + GMLP_W])
    mu = jnp.mean(v, axis=-1, keepdims=True)
    vc = v - mu
    vn = vc * lax.rsqrt(jnp.mean(vc * vc, axis=-1, keepdims=True) + EPS) * gng_ref[...] + gnb_ref[...]
    if emit_vn:
        vn_out[...] = vn.reshape(nseq, ls, GMLP_W)
    vn_bf = vn.astype(BF)
    gq = min(ls, GMLP_CHUNK)
    tri_g = (lax.broadcasted_iota(jnp.int32, (gq, gq), 0)
             >= lax.broadcasted_iota(jnp.int32, (gq, gq), 1))
    wsm = [jnp.where(tri_g, ws_ref[h, 0:gq, 0:gq], jnp.zeros((), BF)) for h in range(GMLP_HEADS)]
    bias_g = bs_ref[0:gq, :]
    sv_rows = []
    for c in range(rows // gq):
        sv_h = [_dot(wsm[h], vn_bf[c * gq:(c + 1) * gq, h * GMLP_HC:(h + 1) * GMLP_HC])
                for h in range(GMLP_HEADS)]
        sv_rows.append(jnp.concatenate(sv_h, axis=1) + bias_g)
    sv = sv_rows[0] if len(sv_rows) == 1 else jnp.concatenate(sv_rows, axis=0)
    yc = u * sv
    merged = merged + (_sigmoid(_dot(xn, win_ref[:, C_GC:C_GC + D_MODEL]))
                       * _dot(yc.astype(BF), wc_ref[...]))

    xbc_pre = _dot(xn, win_ref[:, C_XBC:C_XBC + XBC_W])
    for s in range(nseq):
        ext_conv[s, CONV_PAD:CONV_PAD + ls, :] = xbc_pre[s * ls:(s + 1) * ls]
    for s in range(nseq):
        base = CONV_PAD - (SSM_CONV - 1)
        cv = convb_ref[...] + convw_ref[0:1, :] * ext_conv[s, base:base + ls, :]
        for k in range(1, SSM_CONV):
            cv = cv + convw_ref[k:k + 1, :] * ext_conv[s, base + k:base + k + ls, :]
        act_ref[s * ls:(s + 1) * ls, :] = _silu(cv)

    dt = _softplus(_dot(xn, win_ref[:, C_DT:C_DT + LANES]) + dtb_ref[...])
    d_a = dt * (-jnp.exp(alog_ref[...]))
    ri = lax.broadcasted_iota(jnp.int32, (seg, seg), 0)
    ci = lax.broadcasted_iota(jnp.int32, (seg, seg), 1)
    seg_tri = jnp.where(ri >= ci, 1.0, 0.0).astype(BF)
    cums = []
    for q in range(rows // seg):
        parts = [_dot(seg_tri, p) for p in _split_bf16(d_a[q * seg:(q + 1) * seg, :], 3)]
        cums.append(parts[0] + parts[1] + parts[2])
    cum = cums[0] if len(cums) == 1 else jnp.concatenate(cums, axis=0)
    ei = lax.broadcasted_iota(jnp.int32, (LANES, SSM_INNER), 0)
    ej = lax.broadcasted_iota(jnp.int32, (LANES, SSM_INNER), 1)
    expand = jnp.where(_div_pow2(ej, SSM_HEADDIM) == ei, 1.0, 0.0).astype(BF)

    def _expand(a, parts):
        out = None
        for p in _split_bf16(a, parts):
            e = _dot(p, expand)
            out = e if out is None else out + e
        return out

    nseg = rows // seg
    lasts = [cum[(q + 1) * seg - 1:(q + 1) * seg, :] for q in range(nseg)]
    fxw = [jnp.exp(lasts[q] - cum[q * seg:(q + 1) * seg, :]) * dt[q * seg:(q + 1) * seg, :]
           for q in range(nseg)]
    fxw = fxw[0] if nseg == 1 else jnp.concatenate(fxw, axis=0)
    fac_xw_e = _expand(fxw, 1)
    fac_off_e = _expand(jnp.exp(cum), 1)
    assert nseg <= SUBLANES
    sub = lax.broadcasted_iota(jnp.int32, (SUBLANES, LANES), 0)
    last_rows = jnp.zeros((SUBLANES, LANES), F32)
    for q in range(nseg):
        last_rows = jnp.where(sub == q, lasts[q], last_rows)
    dec_all = _expand(jnp.exp(last_rows), 3)
    dec_e = [dec_all[q:q + 1, :] for q in range(nseg)]
    cum_t = cum.T
    dt_t = dt.T

    bi = lax.broadcasted_iota(jnp.int32, (SSD_BLOCK, SSD_BLOCK), 0)
    bj = lax.broadcasted_iota(jnp.int32, (SSD_BLOCK, SSD_BLOCK), 1)
    blk_tri = (_div_pow2(bi, seg) == _div_pow2(bj, seg)) & (bi >= bj)
    lane_lo = lax.broadcasted_iota(jnp.int32, (SSD_BLOCK, LANES), 1) < SSM_HEADDIM
    nblk = rows // SSD_BLOCK
    segs_per_blk = SSD_BLOCK // seg
    for b in range(nblk):
        r0 = b * SSD_BLOCK
        xs_bf = act_ref[r0:r0 + SSD_BLOCK, 0:SSM_INNER].astype(BF)
        bm_bf = act_ref[r0:r0 + SSD_BLOCK, SSM_INNER:SSM_INNER + SSM_GROUPS * SSM_STATE].astype(BF)
        cm_bf = act_ref[r0:r0 + SSD_BLOCK, SSM_INNER + SSM_GROUPS * SSM_STATE:XBC_W].astype(BF)
        for g in range(SSM_GROUPS):
            c_g = cm_bf[:, g * SSM_STATE:(g + 1) * SSM_STATE]
            b_g = bm_bf[:, g * SSM_STATE:(g + 1) * SSM_STATE]
            cb = jnp.where(blk_tri, _dot_nt(c_g, b_g), 0.0)
            for j in range(2):
                col = (2 * g + j) * LANES
                xs_pair = xs_bf[:, col:col + LANES]
                yd = []
                for hh in range(2):
                    h = 4 * g + 2 * j + hh
                    diff = cum[r0:r0 + SSD_BLOCK, h:h + 1] - cum_t[h:h + 1, r0:r0 + SSD_BLOCK]
                    dec = jnp.exp(jnp.where(blk_tri, diff, -jnp.inf))
                    mm = (cb * dec * dt_t[h:h + 1, r0:r0 + SSD_BLOCK]).astype(BF)
                    yd.append(_dot(mm, xs_pair))
                y_ref[r0:r0 + SSD_BLOCK, col:col + LANES] = jnp.where(lane_lo, yd[0], yd[1])
        for q in range(segs_per_blk):
            q0 = r0 + q * seg
            s = q0 // ls
            xs_q = act_ref[q0:q0 + seg, 0:SSM_INNER]
            xw = (xs_q * fac_xw_e[q0:q0 + seg, :]).astype(BF)
            fac_off = fac_off_e[q0:q0 + seg, :]
            dec_row = dec_e[q0 // seg]
            for g in range(SSM_GROUPS):
                gc0 = g * 4 * SSM_HEADDIM
                gc1 = gc0 + 4 * SSM_HEADDIM
                c_g = cm_bf[q0 - r0:q0 - r0 + seg, g * SSM_STATE:(g + 1) * SSM_STATE]
                b_g = bm_bf[q0 - r0:q0 - r0 + seg, g * SSM_STATE:(g + 1) * SSM_STATE]
                st_g = st_ref[s, :, gc0:gc1]
                y_off = _dot(c_g, st_g.astype(BF)) * fac_off[:, gc0:gc1]
                y_ref[q0:q0 + seg, gc0:gc1] = y_ref[q0:q0 + seg, gc0:gc1] + y_off
                st_ref[s, :, gc0:gc1] = st_g * dec_row[:, gc0:gc1] + _dot_tn(b_g, xw[:, gc0:gc1])

    xs = act_ref[:, 0:SSM_INNER]
    z = _dot(xn, win_ref[:, C_Z:C_Z + SSM_INNER])
    yb = _rms((y_ref[...] + dskip_ref[...] * xs) * _silu(z), normg_ref[...])
    merged = merged + (_sigmoid(_dot(xn, win_ref[:, C_GB:C_GB + D_MODEL]))
                       * _dot(yb.astype(BF), wb_ref[...]))

    m = _dot(merged.astype(BF), wo_ref[...])
    o_ref[...] = (x + _rms(m, post_ref[...])).reshape(nseq, ls, D_MODEL)

    pool_tail = ext_pool[:, ls:ls + POOL_PAD, :]
    conv_tail = ext_conv[:, ls:ls + CONV_PAD, :]
    pool_out[...] = pool_tail
    conv_out[...] = conv_tail
    ext_pool[:, 0:POOL_PAD, :] = pool_tail
    ext_conv[:, 0:CONV_PAD, :] = conv_tail

    @pl.when(t == pl.num_programs(1) - 1)
    def _():
        for s in range(nseq):
            ssm_out[s] = st_ref[s].T


def _mixer_call(x, pool_st, conv_st, ssm_st, pos0, mw, layer, nseq, ls, emit_vn):
    bsz, length, _ = x.shape
    grid = (bsz // nseq, length // ls)
    rows = nseq * ls

    def per_seq(shape):
        return pl.BlockSpec((nseq,) + shape, lambda b, t: (b,) + (0,) * len(shape))

    in_specs = [
        pl.BlockSpec((nseq, ls, D_MODEL), lambda b, t: (b, t, 0)),
        per_seq((POOL_PAD, POOL_W)),
        per_seq((CONV_PAD, XBC_W)),
        per_seq((SSM_INNER, SSM_STATE)),
        _resident((1, D_MODEL), layer),
        _resident((1, D_MODEL), layer),
        _resident((D_MODEL, IN_COLS_R), layer),
        _resident((POOL_W, POOL_W), layer),
        _resident((1, POOL_W), layer),
        _resident((SSM_CONV, XBC_W), layer),
        _resident((1, XBC_W), layer),
        _resident((1, LANES), layer),
        _resident((1, LANES), layer),
        _resident((1, SSM_INNER), layer),
        _resident((1, SSM_INNER), layer),
        _resident((1, GMLP_W), layer),
        _resident((1, GMLP_W), layer),
        _resident((GMLP_HEADS, GMLP_CHUNK, GMLP_CHUNK), layer),
        _resident((GMLP_CHUNK, GMLP_W), layer),
        _resident((POOL_W, D_MODEL), layer),
        _resident((SSM_INNER, D_MODEL), layer),
        _resident((GMLP_W, D_MODEL), layer),
        _resident((D_MODEL, D_MODEL), layer),
    ]
    out_specs = [
        pl.BlockSpec((nseq, ls, D_MODEL), lambda b, t: (b, t, 0)),
        per_seq((POOL_PAD, POOL_W)),
        per_seq((CONV_PAD, XBC_W)),
        per_seq((SSM_INNER, SSM_STATE)),
    ]
    out_shape = [
        jax.ShapeDtypeStruct((bsz, length, D_MODEL), F32),
        jax.ShapeDtypeStruct((bsz, POOL_PAD, POOL_W), F32),
        jax.ShapeDtypeStruct((bsz, CONV_PAD, XBC_W), F32),
        jax.ShapeDtypeStruct((bsz, SSM_INNER, SSM_STATE), F32),
    ]
    if emit_vn:
        out_specs.append(pl.BlockSpec((nseq, ls, GMLP_W), lambda b, t: (b, t, 0)))
        out_shape.append(jax.ShapeDtypeStruct((bsz, length, GMLP_W), F32))
    scratch = [
        pltpu.VMEM((nseq, POOL_PAD + ls, POOL_W), F32),
        pltpu.VMEM((nseq, CONV_PAD + ls, XBC_W), F32),
        pltpu.VMEM((nseq, SSM_STATE, SSM_INNER), F32),
        pltpu.VMEM((rows, XBC_W), F32),
        pltpu.VMEM((rows, SSM_INNER), F32),
    ]
    return pl.pallas_call(
        functools.partial(_mixer_body, nseq, ls, pos0, emit_vn),
        grid=grid,
        in_specs=in_specs,
        out_specs=out_specs,
        out_shape=out_shape,
        scratch_shapes=scratch,
        compiler_params=pltpu.CompilerParams(
            dimension_semantics=("parallel", "arbitrary"), vmem_limit_bytes=VMEM_LIMIT),
        name="mixer_block",
    )(x, pool_st, conv_st, ssm_st, *mw)


def _prep_weights(ffn1_pre_g, ffn1_post_g, ffn1_w_gu, ffn1_w_down,
                  mix_pre_g, mix_post_g, w_in, pool_w, pool_scale,
                  ssm_conv_w, ssm_conv_b, ssm_dt_bias, ssm_a_log, ssm_d, ssm_norm_g,
                  gmlp_norm_g, gmlp_norm_b, gmlp_ws, gmlp_bs,
                  w_branch_a, w_branch_b, w_branch_c, w_out,
                  ffn2_pre_g, ffn2_post_g, ffn2_w_gu, ffn2_w_down):
    row = lambda a: a[:, None, :]
    cuts = np.cumsum([POOL_W, SSM_INNER, XBC_W, SSM_HEADS, GMLP_W, GMLP_W]).tolist()
    xa, z, xbc, dt_c, u, v, gates = jnp.split(w_in, cuts, axis=-1)
    dt_c = jnp.pad(dt_c, ((0, 0), (0, 0), (0, LANES - SSM_HEADS)))
    w_in_r = jnp.concatenate([xa, z, xbc, u, v, gates, dt_c], axis=-1).astype(BF)
    pool_bd = jnp.zeros((DEPTH, POOL_W, POOL_W), F32)
    for g in range(POOL_GROUPS):
        sl = slice(g * POOL_GC, (g + 1) * POOL_GC)
        pool_bd = pool_bd.at[:, sl, sl].set(pool_w[:, g])
    pad_heads = lambda a: row(jnp.pad(a, ((0, 0), (0, LANES - SSM_HEADS))))
    bs_exp = jnp.repeat(jnp.swapaxes(gmlp_bs, 1, 2), GMLP_HC, axis=2)
    ffn1 = (row(ffn1_pre_g), row(ffn1_post_g), ffn1_w_gu.astype(BF), ffn1_w_down.astype(BF))
    ffn2 = (row(ffn2_pre_g), row(ffn2_post_g), ffn2_w_gu.astype(BF), ffn2_w_down.astype(BF))
    mix = (row(mix_pre_g), row(mix_post_g), w_in_r, pool_bd.astype(BF), row(pool_scale),
           ssm_conv_w, row(ssm_conv_b), pad_heads(ssm_dt_bias), pad_heads(ssm_a_log),
           row(jnp.repeat(ssm_d, SSM_HEADDIM, axis=1)), row(ssm_norm_g),
           row(gmlp_norm_g), row(gmlp_norm_b), gmlp_ws.astype(BF), bs_exp,
           w_branch_a.astype(BF), w_branch_b.astype(BF), w_branch_c.astype(BF), w_out.astype(BF))
    return ffn1, mix, ffn2


def _run_trunk(x, pool_st, conv_st, ssm_st, pos0, ffn1, mix, ffn2, tm, nseq, ls, emit_vn):
    bsz, length, _ = x.shape
    depth = pool_st.shape[0]
    pool_p = jnp.pad(pool_st, ((0, 0), (0, 0), (POOL_PAD - POOL_HIST, 0), (0, 0)))
    conv_p = jnp.pad(conv_st, ((0, 0), (0, 0), (CONV_PAD - (SSM_CONV - 1), 0), (0, 0)))
    ssm_p = ssm_st.reshape(depth, bsz, SSM_INNER, SSM_STATE)
    new_pool, new_conv, new_ssm, new_v = [], [], [], []
    h = x
    for l in range(depth):
        h = _ffn_call(h.reshape(bsz * length, D_MODEL), *ffn1, l, tm).reshape(bsz, length, D_MODEL)
        outs = _mixer_call(h, pool_p[l], conv_p[l], ssm_p[l], pos0, mix, l, nseq, ls, emit_vn)
        h = outs[0]
        new_pool.append(outs[1][:, POOL_PAD - POOL_HIST:])
        new_conv.append(outs[2][:, CONV_PAD - (SSM_CONV - 1):])
        new_ssm.append(outs[3].reshape(bsz, SSM_HEADS, SSM_HEADDIM, SSM_STATE))
        if emit_vn:
            new_v.append(outs[4])
        h = _ffn_call(h.reshape(bsz * length, D_MODEL), *ffn2, l, tm).reshape(bsz, length, D_MODEL)
    return h, jnp.stack(new_pool), jnp.stack(new_conv), jnp.stack(new_ssm), new_v


def kernel(x_prompt, x_sample, state_pool, state_conv, state_ssm, ffn1_pre_g, ffn1_post_g, ffn1_w_gu, ffn1_w_down, mix_pre_g, mix_post_g, w_in, pool_w, pool_scale, ssm_conv_w, ssm_conv_b, ssm_dt_bias, ssm_a_log, ssm_d, ssm_norm_g, gmlp_norm_g, gmlp_norm_b, gmlp_ws, gmlp_bs, w_branch_a, w_branch_b, w_branch_c, w_out, ffn2_pre_g, ffn2_post_g, ffn2_w_gu, ffn2_w_down):
    ffn1, mix, ffn2 = _prep_weights(
        ffn1_pre_g, ffn1_post_g, ffn1_w_gu, ffn1_w_down, mix_pre_g, mix_post_g, w_in, pool_w,
        pool_scale, ssm_conv_w, ssm_conv_b, ssm_dt_bias, ssm_a_log, ssm_d, ssm_norm_g,
        gmlp_norm_g, gmlp_norm_b, gmlp_ws, gmlp_bs, w_branch_a, w_branch_b, w_branch_c, w_out,
        ffn2_pre_g, ffn2_post_g, ffn2_w_gu, ffn2_w_down)
    bp = x_prompt.shape[0]
    bs, ls = x_sample.shape[0], x_sample.shape[1]
    zero_pool = jnp.zeros((DEPTH, bp, POOL_HIST, POOL_W), F32)
    zero_conv = jnp.zeros((DEPTH, bp, SSM_CONV - 1, XBC_W), F32)
    zero_ssm = jnp.zeros((DEPTH, bp, SSM_HEADS, SSM_HEADDIM, SSM_STATE), F32)
    y_p, pool_p, conv_p, ssm_p, _ = _run_trunk(
        x_prompt, zero_pool, zero_conv, zero_ssm, 0, ffn1, mix, ffn2,
        tm=PROMPT_FFN_ROWS, nseq=PROMPT_MIX_SEQS, ls=PROMPT_MIX_ROWS, emit_vn=False)
    y_s, pool_s, conv_s, ssm_s, v_rows = _run_trunk(
        x_sample, state_pool, state_conv, state_ssm, PAST_LEN, ffn1, mix, ffn2,
        tm=bs * ls, nseq=bs, ls=ls, emit_vn=True)
    return (y_p, y_s, pool_p, conv_p, ssm_p, pool_s, conv_s, ssm_s, jnp.stack(v_rows))
```

```python
import functools

import numpy as np
import jax
import jax.numpy as jnp
from jax import lax
from jax.experimental import pallas as pl
from jax.experimental.pallas import tpu as pltpu

D_MODEL = 1024
DEPTH = 4
D_FF = 2816
EPS = 1e-6
POOL_W = 512
POOL_GROUPS = 4
POOL_GC = 128
POOL_WINDOWS = (2, 4, 8, 16)
POOL_HIST = 15
SSM_INNER = 1024
SSM_HEADDIM = 64
SSM_HEADS = 16
SSM_GROUPS = 4
SSM_STATE = 128
SSM_CONV = 4
XBC_W = 2048
GMLP_W = 512
GMLP_HEADS = 4
GMLP_HC = 128
GMLP_CHUNK = 128
PAST_LEN = 4096

LANES = 128
SUBLANES = 8
POOL_PAD = 16
CONV_PAD = 8
SSD_BLOCK = 128
FF_COLS = 256
VMEM_LIMIT = 60 * 1024 * 1024
PROMPT_FFN_ROWS = 1024
PROMPT_MIX_SEQS = 1
PROMPT_MIX_ROWS = 512

C_XA = 0
C_Z = C_XA + POOL_W
C_XBC = C_Z + SSM_INNER
C_U = C_XBC + XBC_W
C_V = C_U + GMLP_W
C_GA = C_V + GMLP_W
C_GB = C_GA + D_MODEL
C_GC = C_GB + D_MODEL
C_DT = C_GC + D_MODEL
IN_COLS_R = C_DT + LANES

BF = jnp.bfloat16
F32 = jnp.float32


def _dot(a, b):
    return jnp.dot(a, b, preferred_element_type=F32)


def _dot_nt(a, b):
    return lax.dot_general(a, b, (((1,), (1,)), ((), ())), preferred_element_type=F32)


def _dot_tn(a, b):
    return lax.dot_general(a, b, (((0,), (0,)), ((), ())), preferred_element_type=F32)


def _rms(x, g):
    return x * lax.rsqrt(jnp.mean(x * x, axis=-1, keepdims=True) + EPS) * g


def _sigmoid(x):
    return 0.5 * jnp.tanh(0.5 * x) + 0.5


def _silu(x):
    hx = 0.5 * x
    return hx * jnp.tanh(hx) + hx


def _softplus(x):
    return jnp.maximum(x, 0.0) + jnp.log1p(jnp.exp(-jnp.abs(x)))


def _div_pow2(i, d):
    assert d & (d - 1) == 0
    return jnp.right_shift(i, d.bit_length() - 1)


def _split_bf16(x, parts):
    out = []
    for _ in range(parts - 1):
        p = x.astype(BF)
        out.append(p)
        x = x - p.astype(F32)
    out.append(x.astype(BF))
    return out


def _ffn_body(x_ref, pre_ref, post_ref, wgu_ref, wd_ref, o_ref, h_ref):
    x = x_ref[...]
    xn = _rms(x, pre_ref[...]).astype(BF)
    for j in range(D_FF // FF_COLS):
        g = _dot(xn, wgu_ref[:, j * FF_COLS:(j + 1) * FF_COLS])
        u = _dot(xn, wgu_ref[:, D_FF + j * FF_COLS:D_FF + (j + 1) * FF_COLS])
        h_ref[:, j * FF_COLS:(j + 1) * FF_COLS] = (_silu(g) * u).astype(BF)
    f = _dot(h_ref[...], wd_ref[...])
    o_ref[...] = x + 0.5 * _rms(f, post_ref[...])


def _resident(shape, layer):
    nd = len(shape)
    return pl.BlockSpec((None,) + tuple(shape), lambda *_: (layer,) + (0,) * nd,
                        pipeline_mode=pl.Buffered(1))


def _ffn_call(x2d, pre_g, post_g, w_gu, w_down, layer, tm):
    n = x2d.shape[0]
    return pl.pallas_call(
        _ffn_body,
        grid=(n // tm,),
        in_specs=[
            pl.BlockSpec((tm, D_MODEL), lambda i: (i, 0)),
            _resident((1, D_MODEL), layer),
            _resident((1, D_MODEL), layer),
            _resident((D_MODEL, 2 * D_FF), layer),
            _resident((D_FF, D_MODEL), layer),
        ],
        out_specs=pl.BlockSpec((tm, D_MODEL), lambda i: (i, 0)),
        out_shape=jax.ShapeDtypeStruct((n, D_MODEL), F32),
        scratch_shapes=[pltpu.VMEM((tm, D_FF), BF)],
        compiler_params=pltpu.CompilerParams(
            dimension_semantics=("parallel",), vmem_limit_bytes=VMEM_LIMIT),
        name="swiglu_block",
    )(x2d, pre_g, post_g, w_gu, w_down)


def _mixer_body(nseq, ls, pos0, emit_vn,
                x_ref, pool_in, conv_in, ssm_in, pre_ref, post_ref, win_ref,
                poolw_ref, pscale_ref, convw_ref, convb_ref, dtb_ref, alog_ref, dskip_ref,
                normg_ref, gng_ref, gnb_ref, ws_ref, bs_ref, wa_ref, wb_ref, wc_ref, wo_ref,
                *rest):
    if emit_vn:
        o_ref, pool_out, conv_out, ssm_out, vn_out = rest[:5]
        ext_pool, ext_conv, st_ref, act_ref, y_ref = rest[5:]
    else:
        o_ref, pool_out, conv_out, ssm_out = rest[:4]
        vn_out = None
        ext_pool, ext_conv, st_ref, act_ref, y_ref = rest[4:]

    rows = nseq * ls
    t = pl.program_id(1)
    seg = min(ls, SSD_BLOCK)

    @pl.when(t == 0)
    def _():
        ext_pool[:, 0:POOL_PAD, :] = pool_in[...]
        ext_conv[:, 0:CONV_PAD, :] = conv_in[...]
        for s in range(nseq):
            st_ref[s] = ssm_in[s].T

    x = x_ref[...].reshape(rows, D_MODEL)
    xn = _rms(x, pre_ref[...]).astype(BF)

    xa = _dot(xn, win_ref[:, C_XA:C_XA + POOL_W])
    for s in range(nseq):
        ext_pool[s, POOL_PAD:POOL_PAD + ls, :] = xa[s * ls:(s + 1) * ls]
    pos = pos0 + t * ls + lax.broadcasted_iota(jnp.int32, (ls, 1), 0)
    zz_rows = []
    for s in range(nseq):
        zz_g = []
        for gi, w in enumerate(POOL_WINDOWS):
            c0 = gi * POOL_GC
            acc = ext_pool[s, POOL_PAD:POOL_PAD + ls, c0:c0 + POOL_GC]
            for k in range(1, w):
                acc = acc + ext_pool[s, POOL_PAD - k:POOL_PAD - k + ls, c0:c0 + POOL_GC]
            inv = 1.0 / jnp.minimum(pos + 1, w).astype(F32)
            zz_g.append(acc * inv - ext_pool[s, POOL_PAD:POOL_PAD + ls, c0:c0 + POOL_GC])
        zz_rows.append(jnp.concatenate(zz_g, axis=1))
    zz = zz_rows[0] if nseq == 1 else jnp.concatenate(zz_rows, axis=0)
    ya = _dot(zz.astype(BF), poolw_ref[...]) * pscale_ref[...]
    ya_bf = ya.astype(BF)

    u = _dot(xn, win_ref[:, C_U:C_U + GMLP_W])
    v = _dot(xn, win_ref[:, C_V:C_V + GMLP_W])
    mu = jnp.mean(v, axis=-1, keepdims=True)
    vc = v - mu
    vn = vc * lax.rsqrt(jnp.mean(vc * vc, axis=-1, keepdims=True) + EPS) * gng_ref[...] + gnb_ref[...]
    if emit_vn:
        vn_out[...] = vn.reshape(nseq, ls, GMLP_W)
    vn_bf = vn.astype(BF)
    gq = min(ls, GMLP_CHUNK)
    tri_g = (lax.broadcasted_iota(jnp.int32, (gq, gq), 0)
             >= lax.broadcasted_iota(jnp.int32, (gq, gq), 1))
    wsm = [jnp.where(tri_g, ws_ref[h, 0:gq, 0:gq], jnp.zeros((), BF)) for h in range(GMLP_HEADS)]
    bias_g = bs_ref[0:gq, :]
    sv_rows = []
    for c in range(rows // gq):
        sv_h = [_dot(wsm[h], vn_bf[c * gq:(c + 1) * gq, h * GMLP_HC:(h + 1) * GMLP_HC])
                for h in range(GMLP_HEADS)]
        sv_rows.append(jnp.concatenate(sv_h, axis=1) + bias_g)
    sv = sv_rows[0] if len(sv_rows) == 1 else jnp.concatenate(sv_rows, axis=0)
    yc_bf = (u * sv).astype(BF)

    xbc_pre = _dot(xn, win_ref[:, C_XBC:C_XBC + XBC_W])
    for s in range(nseq):
        ext_conv[s, CONV_PAD:CONV_PAD + ls, :] = xbc_pre[s * ls:(s + 1) * ls]
    for s in range(nseq):
        base = CONV_PAD - (SSM_CONV - 1)
        cv = convb_ref[...] + convw_ref[0:1, :] * ext_conv[s, base:base + ls, :]
        for k in range(1, SSM_CONV):
            cv = cv + convw_ref[k:k + 1, :] * ext_conv[s, base + k:base + k + ls, :]
        act_ref[s * ls:(s + 1) * ls, :] = _silu(cv)

    dt = _softplus(_dot(xn, win_ref[:, C_DT:C_DT + LANES]) + dtb_ref[...])
    d_a = dt * (-jnp.exp(alog_ref[...]))
    ri = lax.broadcasted_iota(jnp.int32, (seg, seg), 0)
    ci = lax.broadcasted_iota(jnp.int32, (seg, seg), 1)
    seg_tri = jnp.where(ri >= ci, 1.0, 0.0).astype(BF)
    cums = []
    for q in range(rows // seg):
        parts = [_dot(seg_tri, p) for p in _split_bf16(d_a[q * seg:(q + 1) * seg, :], 3)]
        cums.append(parts[0] + parts[1] + parts[2])
    cum = cums[0] if len(cums) == 1 else jnp.concatenate(cums, axis=0)
    ei = lax.broadcasted_iota(jnp.int32, (LANES, SSM_INNER), 0)
    ej = lax.broadcasted_iota(jnp.int32, (LANES, SSM_INNER), 1)
    expand = jnp.where(_div_pow2(ej, SSM_HEADDIM) == ei, 1.0, 0.0).astype(BF)

    def _expand(a, parts):
        out = None
        for p in _split_bf16(a, parts):
            e = _dot(p, expand)
            out = e if out is None else out + e
        return out

    nseg = rows // seg
    lasts = [cum[(q + 1) * seg - 1:(q + 1) * seg, :] for q in range(nseg)]
    fxw = [jnp.exp(lasts[q] - cum[q * seg:(q + 1) * seg, :]) * dt[q * seg:(q + 1) * seg, :]
           for q in range(nseg)]
    fxw = fxw[0] if nseg == 1 else jnp.concatenate(fxw, axis=0)
    fac_xw_e = _expand(fxw, 1)
    fac_off_e = _expand(jnp.exp(cum), 1)
    assert nseg <= SUBLANES
    sub = lax.broadcasted_iota(jnp.int32, (SUBLANES, LANES), 0)
    last_rows = jnp.zeros((SUBLANES, LANES), F32)
    for q in range(nseg):
        last_rows = jnp.where(sub == q, lasts[q], last_rows)
    dec_all = _expand(jnp.exp(last_rows), 3)
    dec_e = [dec_all[q:q + 1, :] for q in range(nseg)]
    cum_t = cum.T
    dt_t = dt.T

    bi = lax.broadcasted_iota(jnp.int32, (SSD_BLOCK, SSD_BLOCK), 0)
    bj = lax.broadcasted_iota(jnp.int32, (SSD_BLOCK, SSD_BLOCK), 1)
    blk_tri = (_div_pow2(bi, seg) == _div_pow2(bj, seg)) & (bi >= bj)
    lane_lo = lax.broadcasted_iota(jnp.int32, (SSD_BLOCK, LANES), 1) < SSM_HEADDIM
    nblk = rows // SSD_BLOCK
    segs_per_blk = SSD_BLOCK // seg
    for b in range(nblk):
        r0 = b * SSD_BLOCK
        xs_bf = act_ref[r0:r0 + SSD_BLOCK, 0:SSM_INNER].astype(BF)
        bm_bf = act_ref[r0:r0 + SSD_BLOCK, SSM_INNER:SSM_INNER + SSM_GROUPS * SSM_STATE].astype(BF)
        cm_bf = act_ref[r0:r0 + SSD_BLOCK, SSM_INNER + SSM_GROUPS * SSM_STATE:XBC_W].astype(BF)
        for g in range(SSM_GROUPS):
            c_g = cm_bf[:, g * SSM_STATE:(g + 1) * SSM_STATE]
            b_g = bm_bf[:, g * SSM_STATE:(g + 1) * SSM_STATE]
            cb = jnp.where(blk_tri, _dot_nt(c_g, b_g), 0.0)
            for j in range(2):
                col = (2 * g + j) * LANES
                xs_pair = xs_bf[:, col:col + LANES]
                yd = []
                for hh in range(2):
                    h = 4 * g + 2 * j + hh
                    diff = cum[r0:r0 + SSD_BLOCK, h:h + 1] - cum_t[h:h + 1, r0:r0 + SSD_BLOCK]
                    dec = jnp.exp(jnp.where(blk_tri, diff, -jnp.inf))
                    mm = (cb * dec * dt_t[h:h + 1, r0:r0 + SSD_BLOCK]).astype(BF)
                    yd.append(_dot(mm, xs_pair))
                y_ref[r0:r0 + SSD_BLOCK, col:col + LANES] = jnp.where(lane_lo, yd[0], yd[1])
        for q in range(segs_per_blk):
            q0 = r0 + q * seg
            s = q0 // ls
            xs_q = act_ref[q0:q0 + seg, 0:SSM_INNER]
            xw = (xs_q * fac_xw_e[q0:q0 + seg, :]).astype(BF)
            fac_off = fac_off_e[q0:q0 + seg, :]
            dec_row = dec_e[q0 // seg]
            for g in range(SSM_GROUPS):
                gc0 = g * 4 * SSM_HEADDIM
                gc1 = gc0 + 4 * SSM_HEADDIM
                c_g = cm_bf[q0 - r0:q0 - r0 + seg, g * SSM_STATE:(g + 1) * SSM_STATE]
                b_g = bm_bf[q0 - r0:q0 - r0 + seg, g * SSM_STATE:(g + 1) * SSM_STATE]
                st_g = st_ref[s, :, gc0:gc1]
                y_off = _dot(c_g, st_g.astype(BF)) * fac_off[:, gc0:gc1]
                y_ref[q0:q0 + seg, gc0:gc1] = y_ref[q0:q0 + seg, gc0:gc1] + y_off
                st_ref[s, :, gc0:gc1] = st_g * dec_row[:, gc0:gc1] + _dot_tn(b_g, xw[:, gc0:gc1])

    xs = act_ref[:, 0:SSM_INNER]
    z = _dot(xn, win_ref[:, C_Z:C_Z + SSM_INNER])
    yb = _rms((y_ref[...] + dskip_ref[...] * xs) * _silu(z), normg_ref[...])
    yb_bf = yb.astype(BF)
    merged = []
    for c in range(D_MODEL // FF_COLS):
        sl = slice(c * FF_COLS, (c + 1) * FF_COLS)
        part = None
        for gcol, y_bf, w_ref in ((C_GA, ya_bf, wa_ref), (C_GB, yb_bf, wb_ref), (C_GC, yc_bf, wc_ref)):
            val = (_sigmoid(_dot(xn, win_ref[:, gcol + sl.start:gcol + sl.stop]))
                   * _dot(y_bf, w_ref[:, sl]))
            part = val if part is None else part + val
        merged.append(part.astype(BF))
    m = _dot(jnp.concatenate(merged, axis=1), wo_ref[...])
    o_ref[...] = (x + _rms(m, post_ref[...])).reshape(nseq, ls, D_MODEL)

    pool_tail = ext_pool[:, ls:ls + POOL_PAD, :]
    conv_tail = ext_conv[:, ls:ls + CONV_PAD, :]
    pool_out[...] = pool_tail
    conv_out[...] = conv_tail
    ext_pool[:, 0:POOL_PAD, :] = pool_tail
    ext_conv[:, 0:CONV_PAD, :] = conv_tail

    @pl.when(t == pl.num_programs(1) - 1)
    def _():
        for s in range(nseq):
            ssm_out[s] = st_ref[s].T


def _mixer_call(x, pool_st, conv_st, ssm_st, pos0, mw, layer, nseq, ls, emit_vn):
    bsz, length, _ = x.shape
    grid = (bsz // nseq, length // ls)
    rows = nseq * ls

    def per_seq(shape):
        return pl.BlockSpec((nseq,) + shape, lambda b, t: (b,) + (0,) * len(shape))

    in_specs = [
        pl.BlockSpec((nseq, ls, D_MODEL), lambda b, t: (b, t, 0)),
        per_seq((POOL_PAD, POOL_W)),
        per_seq((CONV_PAD, XBC_W)),
        per_seq((SSM_INNER, SSM_STATE)),
        _resident((1, D_MODEL), layer),
        _resident((1, D_MODEL), layer),
        _resident((D_MODEL, IN_COLS_R), layer),
        _resident((POOL_W, POOL_W), layer),
        _resident((1, POOL_W), layer),
        _resident((SSM_CONV, XBC_W), layer),
        _resident((1, XBC_W), layer),
        _resident((1, LANES), layer),
        _resident((1, LANES), layer),
        _resident((1, SSM_INNER), layer),
        _resident((1, SSM_INNER), layer),
        _resident((1, GMLP_W), layer),
        _resident((1, GMLP_W), layer),
        _resident((GMLP_HEADS, GMLP_CHUNK, GMLP_CHUNK), layer),
        _resident((GMLP_CHUNK, GMLP_W), layer),
        _resident((POOL_W, D_MODEL), layer),
        _resident((SSM_INNER, D_MODEL), layer),
        _resident((GMLP_W, D_MODEL), layer),
        _resident((D_MODEL, D_MODEL), layer),
    ]
    out_specs = [
        pl.BlockSpec((nseq, ls, D_MODEL), lambda b, t: (b, t, 0)),
        per_seq((POOL_PAD, POOL_W)),
        per_seq((CONV_PAD, XBC_W)),
        per_seq((SSM_INNER, SSM_STATE)),
    ]
    out_shape = [
        jax.ShapeDtypeStruct((bsz, length, D_MODEL), F32),
        jax.ShapeDtypeStruct((bsz, POOL_PAD, POOL_W), F32),
        jax.ShapeDtypeStruct((bsz, CONV_PAD, XBC_W), F32),
        jax.ShapeDtypeStruct((bsz, SSM_INNER, SSM_STATE), F32),
    ]
    if emit_vn:
        out_specs.append(pl.BlockSpec((nseq, ls, GMLP_W), lambda b, t: (b, t, 0)))
        out_shape.append(jax.ShapeDtypeStruct((bsz, length, GMLP_W), F32))
    scratch = [
        pltpu.VMEM((nseq, POOL_PAD + ls, POOL_W), F32),
        pltpu.VMEM((nseq, CONV_PAD + ls, XBC_W), F32),
        pltpu.VMEM((nseq, SSM_STATE, SSM_INNER), F32),
        pltpu.VMEM((rows, XBC_W), F32),
        pltpu.VMEM((rows, SSM_INNER), F32),
    ]
    return pl.pallas_call(
        functools.partial(_mixer_body, nseq, ls, pos0, emit_vn),
        grid=grid,
        in_specs=in_specs,
        out_specs=out_specs,
        out_shape=out_shape,
        scratch_shapes=scratch,
        compiler_params=pltpu.CompilerParams(
            dimension_semantics=("parallel", "arbitrary"), vmem_limit_bytes=VMEM_LIMIT),
        name="mixer_block",
    )(x, pool_st, conv_st, ssm_st, *mw)


def _prep_weights(ffn1_pre_g, ffn1_post_g, ffn1_w_gu, ffn1_w_down,
                  mix_pre_g, mix_post_g, w_in, pool_w, pool_scale,
                  ssm_conv_w, ssm_conv_b, ssm_dt_bias, ssm_a_log, ssm_d, ssm_norm_g,
                  gmlp_norm_g, gmlp_norm_b, gmlp_ws, gmlp_bs,
                  w_branch_a, w_branch_b, w_branch_c, w_out,
                  ffn2_pre_g, ffn2_post_g, ffn2_w_gu, ffn2_w_down):
    row = lambda a: a[:, None, :]
    cuts = np.cumsum([POOL_W, SSM_INNER, XBC_W, SSM_HEADS, GMLP_W, GMLP_W]).tolist()
    xa, z, xbc, dt_c, u, v, gates = jnp.split(w_in, cuts, axis=-1)
    dt_c = jnp.pad(dt_c, ((0, 0), (0, 0), (0, LANES - SSM_HEADS)))
    w_in_r = jnp.concatenate([xa, z, xbc, u, v, gates, dt_c], axis=-1).astype(BF)
    pool_bd = jnp.zeros((DEPTH, POOL_W, POOL_W), F32)
    for g in range(POOL_GROUPS):
        sl = slice(g * POOL_GC, (g + 1) * POOL_GC)
        pool_bd = pool_bd.at[:, sl, sl].set(pool_w[:, g])
    pad_heads = lambda a: row(jnp.pad(a, ((0, 0), (0, LANES - SSM_HEADS))))
    bs_exp = jnp.repeat(jnp.swapaxes(gmlp_bs, 1, 2), GMLP_HC, axis=2)
    ffn1 = (row(ffn1_pre_g), row(ffn1_post_g), ffn1_w_gu.astype(BF), ffn1_w_down.astype(BF))
    ffn2 = (row(ffn2_pre_g), row(ffn2_post_g), ffn2_w_gu.astype(BF), ffn2_w_down.astype(BF))
    mix = (row(mix_pre_g), row(mix_post_g), w_in_r, pool_bd.astype(BF), row(pool_scale),
           ssm_conv_w, row(ssm_conv_b), pad_heads(ssm_dt_bias), pad_heads(ssm_a_log),
           row(jnp.repeat(ssm_d, SSM_HEADDIM, axis=1)), row(ssm_norm_g),
           row(gmlp_norm_g), row(gmlp_norm_b), gmlp_ws.astype(BF), bs_exp,
           w_branch_a.astype(BF), w_branch_b.astype(BF), w_branch_c.astype(BF), w_out.astype(BF))
    return ffn1, mix, ffn2


def _run_trunk(x, pool_st, conv_st, ssm_st, pos0, ffn1, mix, ffn2, tm, nseq, ls, emit_vn):
    bsz, length, _ = x.shape
    depth = pool_st.shape[0]
    pool_p = jnp.pad(pool_st, ((0, 0), (0, 0), (POOL_PAD - POOL_HIST, 0), (0, 0)))
    conv_p = jnp.pad(conv_st, ((0, 0), (0, 0), (CONV_PAD - (SSM_CONV - 1), 0), (0, 0)))
    ssm_p = ssm_st.reshape(depth, bsz, SSM_INNER, SSM_STATE)
    new_pool, new_conv, new_ssm, new_v = [], [], [], []
    h = x
    for l in range(depth):
        h = _ffn_call(h.reshape(bsz * length, D_MODEL), *ffn1, l, tm).reshape(bsz, length, D_MODEL)
        outs = _mixer_call(h, pool_p[l], conv_p[l], ssm_p[l], pos0, mix, l, nseq, ls, emit_vn)
        h = outs[0]
        new_pool.append(outs[1][:, POOL_PAD - POOL_HIST:])
        new_conv.append(outs[2][:, CONV_PAD - (SSM_CONV - 1):])
        new_ssm.append(outs[3].reshape(bsz, SSM_HEADS, SSM_HEADDIM, SSM_STATE))
        if emit_vn:
            new_v.append(outs[4])
        h = _ffn_call(h.reshape(bsz * length, D_MODEL), *ffn2, l, tm).reshape(bsz, length, D_MODEL)
    return h, jnp.stack(new_pool), jnp.stack(new_conv), jnp.stack(new_ssm), new_v


def kernel(x_prompt, x_sample, state_pool, state_conv, state_ssm, ffn1_pre_g, ffn1_post_g, ffn1_w_gu, ffn1_w_down, mix_pre_g, mix_post_g, w_in, pool_w, pool_scale, ssm_conv_w, ssm_conv_b, ssm_dt_bias, ssm_a_log, ssm_d, ssm_norm_g, gmlp_norm_g, gmlp_norm_b, gmlp_ws, gmlp_bs, w_branch_a, w_branch_b, w_branch_c, w_out, ffn2_pre_g, ffn2_post_g, ffn2_w_gu, ffn2_w_down):
    ffn1, mix, ffn2 = _prep_weights(
        ffn1_pre_g, ffn1_post_g, ffn1_w_gu, ffn1_w_down, mix_pre_g, mix_post_g, w_in, pool_w,
        pool_scale, ssm_conv_w, ssm_conv_b, ssm_dt_bias, ssm_a_log, ssm_d, ssm_norm_g,
        gmlp_norm_g, gmlp_norm_b, gmlp_ws, gmlp_bs, w_branch_a, w_branch_b, w_branch_c, w_out,
        ffn2_pre_g, ffn2_post_g, ffn2_w_gu, ffn2_w_down)
    bp = x_prompt.shape[0]
    bs, ls = x_sample.shape[0], x_sample.shape[1]
    zero_pool = jnp.zeros((DEPTH, bp, POOL_HIST, POOL_W), F32)
    zero_conv = jnp.zeros((DEPTH, bp, SSM_CONV - 1, XBC_W), F32)
    zero_ssm = jnp.zeros((DEPTH, bp, SSM_HEADS, SSM_HEADDIM, SSM_STATE), F32)
    y_p, pool_p, conv_p, ssm_p, _ = _run_trunk(
        x_prompt, zero_pool, zero_conv, zero_ssm, 0, ffn1, mix, ffn2,
        tm=PROMPT_FFN_ROWS, nseq=PROMPT_MIX_SEQS, ls=PROMPT_MIX_ROWS, emit_vn=False)
    y_s, pool_s, conv_s, ssm_s, v_rows = _run_trunk(
        x_sample, state_pool, state_conv, state_ssm, PAST_LEN, ffn1, mix, ffn2,
        tm=bs * ls, nseq=bs, ls=ls, emit_vn=True)
    return (y_p, y_s, pool_p, conv_p, ssm_p, pool_s, conv_s, ssm_s, jnp.stack(v_rows))
```
